```python
import math
import jax, jax.numpy as jnp
from jax import lax
import numpy as np

D_MODEL = 1024
BATCH = 2
SEQ = 8192
DEPTH = 2
DEC_BATCH = 128
DEC_SEQ = 1
PAST_LEN = 16384
PAGE_SIZE = 128

MOBA_HEADS = 8
MOBA_KV_HEADS = 2
MOBA_HEAD_DIM = 64
MOBA_BLOCK = 256
MOBA_TOPK = 3
MOBA_ROPE_DIM = MOBA_HEAD_DIM // 4
MOBA_SCALE = MOBA_HEAD_DIM ** -0.5
MLA_HEADS = 8
MLA_Q_RANK = 256
MLA_KV_RANK = 128
MLA_NOPE_DIM = 64
MLA_ROPE_DIM = 32
MLA_V_DIM = 64
MLA_SCALE = (MLA_NOPE_DIM + MLA_ROPE_DIM) ** -0.5
DIFF_HEADS = 4
DIFF_KV_HEADS = 1
DIFF_HEAD_DIM = 64
DIFF_ROPE_DIM = DIFF_HEAD_DIM // 4
DIFF_SCALE = DIFF_HEAD_DIM ** -0.5
N_BRANCH = 3
ROPE_THETA = 500000.0
Q_BLOCK = 128
LN_EPS = 1e-5
RMS_EPS = 1e-6
D_FF = 2816
N_EXPERTS = 8
TOP_K = 2
D_FF_EXPERT = 3584
DEEPNORM_ALPHA = (2 * DEPTH) ** 0.25
DEEPNORM_BETA = (8 * DEPTH) ** -0.25
IN_WIDTHS = (MOBA_HEADS * MOBA_HEAD_DIM, MOBA_KV_HEADS * MOBA_HEAD_DIM, MOBA_KV_HEADS * MOBA_HEAD_DIM,
             MLA_Q_RANK, MLA_KV_RANK, MLA_ROPE_DIM,
             DIFF_HEADS * 2 * DIFF_HEAD_DIM, DIFF_KV_HEADS * 2 * DIFF_HEAD_DIM, DIFF_KV_HEADS * 2 * DIFF_HEAD_DIM,
             N_BRANCH * D_MODEL)
D_IN = sum(IN_WIDTHS)

kernel_name = 'hybrid_moba_mla_diff_deepnorm_step'

F32 = jnp.float32


def _split_points():
    pts, acc = [], 0
    for w in IN_WIDTHS[:-1]:
        acc += w
        pts.append(acc)
    return pts


def layer_norm(x, g, b):
    xf = x.astype(F32)
    mu = xf.mean(-1, keepdims=True)
    var = jnp.square(xf - mu).mean(-1, keepdims=True)
    return ((xf - mu) * lax.rsqrt(var + LN_EPS) * g.astype(F32) + b.astype(F32)).astype(x.dtype)


def rms_norm(x, g, eps=RMS_EPS):
    xf = x.astype(F32)
    return (xf * lax.rsqrt(jnp.mean(xf * xf, -1, keepdims=True) + eps) * g.astype(F32)).astype(x.dtype)


def rope_angles(pos, dim):
    inv = ROPE_THETA ** (-jnp.arange(0, dim, 2, dtype=F32) / dim)
    ang = pos.astype(F32)[:, None] * inv[None, :]
    return jnp.cos(ang), jnp.sin(ang)


def apply_rope(x, cos, sin):
    d2 = x.shape[-1] // 2
    x1, x2 = x[..., :d2], x[..., d2:]
    c = cos[:, None, :].astype(x.dtype)
    s = sin[:, None, :].astype(x.dtype)
    return jnp.concatenate([x1 * c - x2 * s, x1 * s + x2 * c], axis=-1)


def partial_rope(x, cos, sin):
    r = 2 * cos.shape[-1]
    return jnp.concatenate([apply_rope(x[..., :r], cos, sin), x[..., r:]], axis=-1)


def project_tokens(x, pos, w_in_l, q_norm_l, w_uq_l, kv_norm_l):
    B, T, _ = x.shape
    mq, mk, mv, cq, ckv, kr, dq, dk, dv, gt = jnp.split(x @ w_in_l, _split_points(), axis=-1)
    cos_a, sin_a = rope_angles(pos, MOBA_ROPE_DIM)
    mq = partial_rope(mq.reshape(B, T, MOBA_HEADS, MOBA_HEAD_DIM), cos_a, sin_a)
    mk = partial_rope(mk.reshape(B, T, MOBA_KV_HEADS, MOBA_HEAD_DIM), cos_a, sin_a)
    mv = mv.reshape(B, T, MOBA_KV_HEADS, MOBA_HEAD_DIM)
    cos_c, sin_c = rope_angles(pos, DIFF_ROPE_DIM)
    dq = partial_rope(dq.reshape(B, T, DIFF_HEADS * 2, DIFF_HEAD_DIM), cos_c, sin_c).reshape(B, T, DIFF_HEADS, 2, DIFF_HEAD_DIM)
    dk = partial_rope(dk.reshape(B, T, DIFF_KV_HEADS * 2, DIFF_HEAD_DIM), cos_c, sin_c).reshape(B, T, DIFF_KV_HEADS, 2, DIFF_HEAD_DIM)
    dv = dv.reshape(B, T, DIFF_KV_HEADS, 2 * DIFF_HEAD_DIM)
    cos_m, sin_m = rope_angles(pos, MLA_ROPE_DIM)
    q = (rms_norm(cq, q_norm_l) @ w_uq_l).reshape(B, T, MLA_HEADS, MLA_NOPE_DIM + MLA_ROPE_DIM)
    q_nope = q[..., :MLA_NOPE_DIM]
    q_rope = apply_rope(q[..., MLA_NOPE_DIM:], cos_m, sin_m)
    ckv = rms_norm(ckv, kv_norm_l)
    kr = apply_rope(kr[:, :, None, :], cos_m, sin_m)[:, :, 0, :]
    gates = jax.nn.sigmoid(gt.astype(F32)).astype(x.dtype).reshape(B, T, N_BRANCH, D_MODEL)
    return mq, mk, mv, q_nope, q_rope, ckv, kr, dq, dk, dv, gates


def map_query_blocks(fn, qs, q_pos):
    nc = q_pos.shape[0] // Q_BLOCK
    def split(a):
        return jnp.moveaxis(a.reshape(a.shape[0], nc, Q_BLOCK, *a.shape[2:]), 1, 0)
    out = lax.map(lambda a: fn(a[0], a[1]), (tuple(split(q) for q in qs), q_pos.reshape(nc, Q_BLOCK)))
    out = jnp.moveaxis(out, 0, 1)
    return out.reshape(out.shape[0], nc * Q_BLOCK, *out.shape[3:])


def moba_blocks(k_parts, v_parts):
    B = k_parts[0].shape[0]
    L = sum(p.shape[1] for p in k_parts)
    nb = -(-L // MOBA_BLOCK)
    pad = nb * MOBA_BLOCK - L
    k = jnp.concatenate(list(k_parts) + [jnp.zeros((B, pad) + k_parts[0].shape[2:], k_parts[-1].dtype)], axis=1)
    v = jnp.concatenate(list(v_parts) + [jnp.zeros((B, pad) + v_parts[0].shape[2:], v_parts[-1].dtype)], axis=1)
    kblk = k.reshape(B, nb, MOBA_BLOCK, MOBA_KV_HEADS, MOBA_HEAD_DIM)
    vblk = v.reshape(B, nb, MOBA_BLOCK, MOBA_KV_HEADS, MOBA_HEAD_DIM)
    kmean = kblk.astype(F32).mean(axis=2).astype(k.dtype)
    return kblk, vblk, kmean


def moba_attend(q, kblk, vblk, kmean, q_pos):
    B, T, H, hd = q.shape
    nb = kblk.shape[1]
    grp = H // MOBA_KV_HEADS
    gate = jnp.einsum('btkgd,bnkd->btkgn', q.reshape(B, T, MOBA_KV_HEADS, grp, hd), kmean).reshape(B, T, H, nb).astype(F32)
    q_blk = q_pos // MOBA_BLOCK
    fully_past = jnp.arange(nb)[None, :] < q_blk[:, None]
    gate = jnp.where(fully_past[None, :, None, :], gate, -jnp.inf)
    sel_score, sel_idx = lax.top_k(gate, min(MOBA_TOPK, nb))
    own = jnp.broadcast_to(q_blk[None, :, None, None], (B, T, H, 1)).astype(sel_idx.dtype)
    idx = jnp.concatenate([sel_idx, own], axis=-1)
    ok = jnp.concatenate([jnp.isfinite(sel_score), jnp.ones((B, T, H, 1), bool)], axis=-1)
    b_ix = jnp.arange(B)[:, None, None, None]
    g_ix = (jnp.arange(H) // grp)[None, None, :, None]
    kg = kblk[b_ix, idx, :, g_ix]
    vg = vblk[b_ix, idx, :, g_ix]
    s = jnp.einsum('bthd,bthsrd->bthsr', q, kg).astype(F32) * MOBA_SCALE
    key_pos = idx[..., None] * MOBA_BLOCK + jnp.arange(MOBA_BLOCK)
    mask = ok[..., None] & (key_pos <= q_pos[None, :, None, None, None])
    p = jax.nn.softmax(jnp.where(mask, s, -jnp.inf).reshape(B, T, H, -1), axis=-1)
    return jnp.einsum('bthn,bthnd->bthd', p.astype(vg.dtype), vg.reshape(B, T, H, -1, hd))


def mla_attend(q_nope, q_rope, ckv, krope, w_uk, w_uv, q_pos):
    q_abs = jnp.einsum('bthd,chd->bthc', q_nope, w_uk)
    s = (jnp.einsum('bthc,blc->bhtl', q_abs, ckv) + jnp.einsum('bthr,blr->bhtl', q_rope, krope)).astype(F32) * MLA_SCALE
    mask = jnp.arange(ckv.shape[1])[None, :] <= q_pos[:, None]
    p = jax.nn.softmax(jnp.where(mask[None, None], s, -jnp.inf), axis=-1)
    o_lat = jnp.einsum('bhtl,blc->bthc', p.astype(ckv.dtype), ckv)
    return jnp.einsum('bthc,chd->bthd', o_lat, w_uv)


def diff_attend(q, k, v, q_pos, lam, subln_g, lam_init):
    B, T = q.shape[:2]
    grp = DIFF_HEADS // DIFF_KV_HEADS
    qg = q.reshape(B, T, DIFF_KV_HEADS, grp, 2, DIFF_HEAD_DIM)
    s = jnp.einsum('btkgcd,blkcd->bkgctl', qg, k).astype(F32) * DIFF_SCALE
    mask = jnp.arange(k.shape[1])[None, :] <= q_pos[:, None]
    p = jax.nn.softmax(jnp.where(mask, s, -jnp.inf), axis=-1)
    a = p[:, :, :, 0] - lam * p[:, :, :, 1]
    o = jnp.einsum('bkgtl,blke->btkge', a.astype(v.dtype), v).reshape(B, T, DIFF_HEADS, 2 * DIFF_HEAD_DIM)
    return rms_norm(o, subln_g, 1e-5) * (1.0 - lam_init)


def merge_branches(o_a, o_b, o_c, gates, w_a, w_b, w_c, w_o):
    B, T = o_a.shape[:2]
    ya = o_a.reshape(B, T, -1) @ w_a
    yb = o_b.reshape(B, T, -1) @ w_b
    yc = o_c.reshape(B, T, -1) @ w_c
    m = gates[:, :, 0] * ya + gates[:, :, 1] * yb + gates[:, :, 2] * yc
    return m @ w_o


def swiglu(x, wg, wu, wd):
    return (jax.nn.silu(x @ wg) * (x @ wu)) @ wd


def moe_swiglu(x, router, wg, wu, wd):
    logits = (x @ router).astype(F32)
    top_v, top_i = lax.top_k(logits, TOP_K)
    w = jax.nn.softmax(top_v, axis=-1)
    gate = jnp.sum(jax.nn.one_hot(top_i, N_EXPERTS, dtype=F32) * w[..., None], axis=-2).astype(x.dtype)
    y = jnp.zeros_like(x)
    for e in range(N_EXPERTS):
        y = y + gate[..., e:e + 1] * swiglu(x, wg[e], wu[e], wd[e])
    return y


def gather_past(cache, l, page_table):
    g = cache[l, page_table]
    return g.reshape(g.shape[0], g.shape[1] * g.shape[2], *g.shape[3:])


def setup_inputs(seed: int = 0) -> dict:
    key = jax.random.key(seed)
    ks = iter(jax.random.split(key, 48))
    def nrm(shape, scale=1.0):
        return jax.random.normal(next(ks), shape, F32) * scale
    n_pages = PAST_LEN // PAGE_SIZE
    n_pool = (DEC_BATCH * n_pages * 5) // 4
    n_dense = (DEPTH + 1) // 2
    n_moe = DEPTH // 2
    perm = jax.random.permutation(next(ks), n_pool)
    page_table = perm[: DEC_BATCH * n_pages].reshape(DEC_BATCH, n_pages).astype(jnp.int32)
    w_mix = MOBA_HEADS * MOBA_HEAD_DIM
    return {
        'x_prompt': nrm((BATCH, SEQ, D_MODEL)),
        'x_sample': nrm((DEC_BATCH, DEC_SEQ, D_MODEL)),
        'cache_moba_k': nrm((DEPTH, n_pool, PAGE_SIZE, MOBA_KV_HEADS, MOBA_HEAD_DIM)),
        'cache_moba_v': nrm((DEPTH, n_pool, PAGE_SIZE, MOBA_KV_HEADS, MOBA_HEAD_DIM)),
        'cache_mla_ckv': nrm((DEPTH, n_pool, PAGE_SIZE, MLA_KV_RANK)),
        'cache_mla_krope': nrm((DEPTH, n_pool, PAGE_SIZE, MLA_ROPE_DIM)),
        'cache_diff_k': nrm((DEPTH, n_pool, PAGE_SIZE, DIFF_KV_HEADS, 2, DIFF_HEAD_DIM)),
        'cache_diff_v': nrm((DEPTH, n_pool, PAGE_SIZE, DIFF_KV_HEADS, 2 * DIFF_HEAD_DIM)),
        'page_table': page_table,
        'w_in': nrm((DEPTH, D_MODEL, D_IN), D_MODEL ** -0.5),
        'mla_q_norm': 1.0 + nrm((DEPTH, MLA_Q_RANK), 0.02),
        'mla_w_uq': nrm((DEPTH, MLA_Q_RANK, MLA_HEADS * (MLA_NOPE_DIM + MLA_ROPE_DIM)), MLA_Q_RANK ** -0.5),
        'mla_kv_norm': 1.0 + nrm((DEPTH, MLA_KV_RANK), 0.02),
        'mla_w_uk': nrm((DEPTH, MLA_KV_RANK, MLA_HEADS, MLA_NOPE_DIM), MLA_KV_RANK ** -0.5),
        'mla_w_uv': nrm((DEPTH, MLA_KV_RANK, MLA_HEADS, MLA_V_DIM), MLA_KV_RANK ** -0.5),
        'diff_lambda_q1': nrm((DEPTH, DIFF_HEAD_DIM), 0.1),
        'diff_lambda_k1': nrm((DEPTH, DIFF_HEAD_DIM), 0.1),
        'diff_lambda_q2': nrm((DEPTH, DIFF_HEAD_DIM), 0.1),
        'diff_lambda_k2': nrm((DEPTH, DIFF_HEAD_DIM), 0.1),
        'diff_subln': 1.0 + nrm((DEPTH, 2 * DIFF_HEAD_DIM), 0.02),
        'w_br_moba': nrm((DEPTH, w_mix, D_MODEL), w_mix ** -0.5),
        'w_br_mla': nrm((DEPTH, MLA_HEADS * MLA_V_DIM, D_MODEL), (MLA_HEADS * MLA_V_DIM) ** -0.5),
        'w_br_diff': nrm((DEPTH, DIFF_HEADS * 2 * DIFF_HEAD_DIM, D_MODEL), (DIFF_HEADS * 2 * DIFF_HEAD_DIM) ** -0.5),
        'w_o': nrm((DEPTH, D_MODEL, D_MODEL), DEEPNORM_BETA * D_MODEL ** -0.5),
        'ln1_g': 1.0 + nrm((DEPTH, D_MODEL), 0.02),
        'ln1_b': nrm((DEPTH, D_MODEL), 0.02),
        'ln2_g': 1.0 + nrm((DEPTH, D_MODEL), 0.02),
        'ln2_b': nrm((DEPTH, D_MODEL), 0.02),
        'ffn_w_gate': nrm((n_dense, D_MODEL, D_FF), D_MODEL ** -0.5),
        'ffn_w_up': nrm((n_dense, D_MODEL, D_FF), D_MODEL ** -0.5),
        'ffn_w_down': nrm((n_dense, D_FF, D_MODEL), DEEPNORM_BETA * D_FF ** -0.5),
        'moe_router': nrm((n_moe, D_MODEL, N_EXPERTS), D_MODEL ** -0.5),
        'moe_w_gate': nrm((n_moe, N_EXPERTS, D_MODEL, D_FF_EXPERT), D_MODEL ** -0.5),
        'moe_w_up': nrm((n_moe, N_EXPERTS, D_MODEL, D_FF_EXPERT), D_MODEL ** -0.5),
        'moe_w_down': nrm((n_moe, N_EXPERTS, D_FF_EXPERT, D_MODEL), DEEPNORM_BETA * D_FF_EXPERT ** -0.5),
    }


def reference(x_prompt, x_sample, cache_moba_k, cache_moba_v, cache_mla_ckv, cache_mla_krope, cache_diff_k, cache_diff_v,
              page_table, w_in, mla_q_norm, mla_w_uq, mla_kv_norm, mla_w_uk, mla_w_uv,
              diff_lambda_q1, diff_lambda_k1, diff_lambda_q2, diff_lambda_k2, diff_subln,
              w_br_moba, w_br_mla, w_br_diff, w_o, ln1_g, ln1_b, ln2_g, ln2_b,
              ffn_w_gate, ffn_w_up, ffn_w_down, moe_router, moe_w_gate, moe_w_up, moe_w_down):
    S = x_prompt.shape[1]
    T = x_sample.shape[1]
    P = page_table.shape[1] * PAGE_SIZE
    pos_p = jnp.arange(S, dtype=jnp.int32)
    pos_s = P + jnp.arange(T, dtype=jnp.int32)
    xp, xs = x_prompt, x_sample
    mk_pl, mk_sl, mv_pl, mv_sl, ck_pl, ck_sl, kr_pl, kr_sl, dk_pl, dk_sl, dv_pl, dv_sl = ([] for _ in range(12))
    for l in range(DEPTH):
        lam_init = 0.8 - 0.6 * math.exp(-0.3 * l)
        lam = (jnp.exp(jnp.sum(diff_lambda_q1[l].astype(F32) * diff_lambda_k1[l].astype(F32)))
               - jnp.exp(jnp.sum(diff_lambda_q2[l].astype(F32) * diff_lambda_k2[l].astype(F32))) + lam_init)
        w_uk, w_uv, subln = mla_w_uk[l], mla_w_uv[l], diff_subln[l]
        (mq_p, mk_p, mv_p, qn_p, qr_p, ck_p, kr_p, dq_p, dk_p, dv_p, g_p) = project_tokens(
            xp, pos_p, w_in[l], mla_q_norm[l], mla_w_uq[l], mla_kv_norm[l])
        (mq_s, mk_s, mv_s, qn_s, qr_s, ck_s, kr_s, dq_s, dk_s, dv_s, g_s) = project_tokens(
            xs, pos_s, w_in[l], mla_q_norm[l], mla_w_uq[l], mla_kv_norm[l])

        kb_p, vb_p, km_p = moba_blocks([mk_p], [mv_p])
        oa_p = map_query_blocks(lambda qs, pos: moba_attend(qs[0], kb_p, vb_p, km_p, pos), (mq_p,), pos_p)
        ob_p = map_query_blocks(lambda qs, pos: mla_attend(qs[0], qs[1], ck_p, kr_p, w_uk, w_uv, pos), (qn_p, qr_p), pos_p)
        oc_p = map_query_blocks(lambda qs, pos: diff_attend(qs[0], dk_p, dv_p, pos, lam, subln, lam_init), (dq_p,), pos_p)

        kb_s, vb_s, km_s = moba_blocks([gather_past(cache_moba_k, l, page_table).astype(mk_s.dtype), mk_s],
                                       [gather_past(cache_moba_v, l, page_table).astype(mv_s.dtype), mv_s])
        oa_s = moba_attend(mq_s, kb_s, vb_s, km_s, pos_s)
        ck_all = jnp.concatenate([gather_past(cache_mla_ckv, l, page_table).astype(ck_s.dtype), ck_s], axis=1)
        kr_all = jnp.concatenate([gather_past(cache_mla_krope, l, page_table).astype(kr_s.dtype), kr_s], axis=1)
        ob_s = mla_attend(qn_s, qr_s, ck_all, kr_all, w_uk, w_uv, pos_s)
        dk_all = jnp.concatenate([gather_past(cache_diff_k, l, page_table).astype(dk_s.dtype), dk_s], axis=1)
        dv_all = jnp.concatenate([gather_past(cache_diff_v, l, page_table).astype(dv_s.dtype), dv_s], axis=1)
        oc_s = diff_attend(dq_s, dk_all, dv_all, pos_s, lam, subln, lam_init)

        xp = layer_norm(DEEPNORM_ALPHA * xp + merge_branches(oa_p, ob_p, oc_p, g_p, w_br_moba[l], w_br_mla[l], w_br_diff[l], w_o[l]), ln1_g[l], ln1_b[l])
        xs = layer_norm(DEEPNORM_ALPHA * xs + merge_branches(oa_s, ob_s, oc_s, g_s, w_br_moba[l], w_br_mla[l], w_br_diff[l], w_o[l]), ln1_g[l], ln1_b[l])

        j = l // 2
        if l % 2 == 0:
            fp = swiglu(xp, ffn_w_gate[j], ffn_w_up[j], ffn_w_down[j])
            fs = swiglu(xs, ffn_w_gate[j], ffn_w_up[j], ffn_w_down[j])
        else:
            fp = moe_swiglu(xp, moe_router[j], moe_w_gate[j], moe_w_up[j], moe_w_down[j])
            fs = moe_swiglu(xs, moe_router[j], moe_w_gate[j], moe_w_up[j], moe_w_down[j])
        xp = layer_norm(DEEPNORM_ALPHA * xp + fp, ln2_g[l], ln2_b[l])
        xs = layer_norm(DEEPNORM_ALPHA * xs + fs, ln2_g[l], ln2_b[l])

        mk_pl.append(mk_p); mk_sl.append(mk_s); mv_pl.append(mv_p); mv_sl.append(mv_s)
        ck_pl.append(ck_p); ck_sl.append(ck_s); kr_pl.append(kr_p); kr_sl.append(kr_s)
        dk_pl.append(dk_p); dk_sl.append(dk_s); dv_pl.append(dv_p); dv_sl.append(dv_s)

    return (xp, xs,
            jnp.stack(mk_pl), jnp.stack(mk_sl), jnp.stack(mv_pl), jnp.stack(mv_sl),
            jnp.stack(ck_pl), jnp.stack(ck_sl), jnp.stack(kr_pl), jnp.stack(kr_sl),
            jnp.stack(dk_pl), jnp.stack(dk_sl), jnp.stack(dv_pl), jnp.stack(dv_sl))
```

```python
import functools
import math

import jax
import jax.numpy as jnp
from jax import lax
from jax.experimental import pallas as pl
from jax.experimental.pallas import tpu as pltpu

F32 = jnp.float32
BF16 = jnp.bfloat16

D_MODEL = 1024
PAGE_SIZE = 128
MOBA_HEADS = 8
MOBA_KV_HEADS = 2
MOBA_HEAD_DIM = 64
MOBA_BLOCK = 256
MOBA_TOPK = 3
MOBA_ROPE_DIM = MOBA_HEAD_DIM // 4
MOBA_SCALE = MOBA_HEAD_DIM ** -0.5
MLA_HEADS = 8
MLA_Q_RANK = 256
MLA_KV_RANK = 128
MLA_NOPE_DIM = 64
MLA_ROPE_DIM = 32
MLA_V_DIM = 64
MLA_SCALE = (MLA_NOPE_DIM + MLA_ROPE_DIM) ** -0.5
DIFF_HEADS = 4
DIFF_KV_HEADS = 1
DIFF_HEAD_DIM = 64
DIFF_ROPE_DIM = DIFF_HEAD_DIM // 4
DIFF_SCALE = DIFF_HEAD_DIM ** -0.5
N_BRANCH = 3
ROPE_THETA = 500000.0
LN_EPS = 1e-5
RMS_EPS = 1e-6
DIFF_SUBLN_EPS = 1e-5
N_EXPERTS = 8
IN_WIDTHS = (MOBA_HEADS * MOBA_HEAD_DIM, MOBA_KV_HEADS * MOBA_HEAD_DIM, MOBA_KV_HEADS * MOBA_HEAD_DIM,
             MLA_Q_RANK, MLA_KV_RANK, MLA_ROPE_DIM,
             DIFF_HEADS * 2 * DIFF_HEAD_DIM, DIFF_KV_HEADS * 2 * DIFF_HEAD_DIM, DIFF_KV_HEADS * 2 * DIFF_HEAD_DIM,
             N_BRANCH * D_MODEL)

LANES = 128
VMEM_LIMIT_BYTES = 56 * 1024 * 1024
N_STREAMS = 8
NEG = -1e30
MOBA_HEAD_ORDER = (0, 4, 1, 5, 2, 6, 3, 7)

_C_MQ, _C_MK, _C_MV, _C_CQ, _C_CKV, _C_KR, _C_DQ, _C_DK, _C_DV, _C_END = (
    0, 512, 640, 768, 1024, 1152, 1280, 1792, 1920, 2048)


def _nt_dot(a, b):
    return lax.dot_general(a, b, (((1,), (1,)), ((), ())), preferred_element_type=F32)


def _dot(a, b):
    return jnp.dot(a, b, preferred_element_type=F32)


def _layer_norm(r, g, b):
    mu = jnp.mean(r, axis=-1, keepdims=True)
    d = r - mu
    var = jnp.mean(d * d, axis=-1, keepdims=True)
    return d * lax.rsqrt(var + LN_EPS) * g + b


def _rope128(h, cos, s1, s2, half):
    return h * cos + pltpu.roll(h, LANES - half, 1) * s1 + pltpu.roll(h, half, 1) * s2


def _fold_kernel(a_ref, b_ref, o_ref):
    o_ref[0] = jnp.dot(a_ref[0], b_ref[0], preferred_element_type=F32, precision=lax.Precision.HIGHEST)


def _fold(a, b):
    nh, m, k = a.shape
    n = b.shape[2]
    return pl.pallas_call(
        _fold_kernel,
        grid=(nh,),
        in_specs=[pl.BlockSpec((1, m, k), lambda h: (h, 0, 0)), pl.BlockSpec((1, k, n), lambda h: (h, 0, 0))],
        out_specs=pl.BlockSpec((1, m, n), lambda h: (h, 0, 0)),
        out_shape=jax.ShapeDtypeStruct((nh, m, n), F32),
        name="fold_weights",
    )(a, b)


def _in_proj_kernel(x_ref, w_ref, wq2_ref, qn_ref, kvn_ref, tab_ref,
                    mq_ref, mk_ref, mv_ref, mkb_ref, mvb_ref, qcat_ref, ckv_ref, kr_ref, kcat_ref,
                    dq_ref, dk_ref, dv_ref, dkb_ref, dvb_ref, *kmean_refs):
    tm = x_ref.shape[0]
    xb = x_ref[...].astype(BF16)
    cos_a, s1_a, s2_a = tab_ref[0], tab_ref[1], tab_ref[2]
    cos_b, s1_b, s2_b = tab_ref[3], tab_ref[4], tab_ref[5]
    half_a = MOBA_ROPE_DIM // 2
    half_b = MLA_ROPE_DIM // 2
    lo = lax.broadcasted_iota(jnp.int32, (tm, LANES), 1) < (LANES // 2)

    def proj(c0, c1):
        return _dot(xb, w_ref[:, c0:c1])

    def rope_a(h):
        return _rope128(h, cos_a, s1_a, s2_a, half_a)

    def expand_halves(h, n_slabs):
        pieces = []
        for v in range(n_slabs):
            hv = rope_a(h[:, v * LANES:(v + 1) * LANES])
            pieces.append(jnp.where(lo, hv, 0.0))
            pieces.append(jnp.where(lo, 0.0, hv))
        return jnp.concatenate(pieces, axis=1).astype(BF16)

    mq_ref[...] = expand_halves(proj(_C_MQ, _C_MK), 4)
    mk = rope_a(proj(_C_MK, _C_MV))
    mk_ref[...] = mk
    mkb_ref[...] = mk.astype(BF16)
    mv = proj(_C_MV, _C_CQ)
    mv_ref[...] = mv
    mvb_ref[...] = mv.astype(BF16)
    if kmean_refs:
        for blk in range(tm // MOBA_BLOCK):
            kmean_refs[0][blk] = jnp.mean(mk[blk * MOBA_BLOCK:(blk + 1) * MOBA_BLOCK], axis=0, keepdims=True)

    cq = proj(_C_CQ, _C_CKV)
    cqn = cq * lax.rsqrt(jnp.mean(cq * cq, axis=-1, keepdims=True) + RMS_EPS) * qn_ref[...]
    q2 = _dot(cqn.astype(BF16), wq2_ref[...])
    pieces = []
    for h in range(MLA_HEADS):
        pieces.append(q2[:, (2 * h) * LANES:(2 * h + 1) * LANES])
        pieces.append(_rope128(q2[:, (2 * h + 1) * LANES:(2 * h + 2) * LANES], cos_b, s1_b, s2_b, half_b))
    qcat_ref[...] = jnp.concatenate(pieces, axis=1).astype(BF16)
    ckv = proj(_C_CKV, _C_KR)
    ckv = ckv * lax.rsqrt(jnp.mean(ckv * ckv, axis=-1, keepdims=True) + RMS_EPS) * kvn_ref[...]
    ckv_ref[...] = ckv
    kr = _rope128(proj(_C_KR, _C_DQ), cos_b, s1_b, s2_b, half_b)
    kr_ref[...] = kr[:, :MLA_ROPE_DIM]
    kcat_ref[...] = jnp.concatenate([ckv, kr], axis=1).astype(BF16)

    dq_ref[...] = expand_halves(proj(_C_DQ, _C_DK), 4)
    dk = rope_a(proj(_C_DK, _C_DV))
    dk_ref[...] = dk
    dkb_ref[...] = dk.astype(BF16)
    dv = proj(_C_DV, _C_END)
    dv_ref[...] = dv
    dvb_ref[...] = dv.astype(BF16)


def _in_proj(x2d, w_proj, w_q2, q_norm, kv_norm, tabs, *, tm, with_kmean):
    n = x2d.shape[0]
    n_pos_tiles = tabs.shape[1] // tm
    row = lambda t: (t, 0)
    const = lambda t: (0, 0)

    def rows(width, dtype):
        return jax.ShapeDtypeStruct((n, width), dtype), pl.BlockSpec((tm, width), row)

    outs = [rows(1024, BF16), rows(128, F32), rows(128, F32), rows(128, BF16), rows(128, BF16),
            rows(2048, BF16), rows(128, F32), rows(MLA_ROPE_DIM, F32), rows(256, BF16),
            rows(1024, BF16), rows(128, F32), rows(128, F32), rows(128, BF16), rows(128, BF16)]
    if with_kmean:
        nblk = tm // MOBA_BLOCK
        outs.append((jax.ShapeDtypeStruct((n // MOBA_BLOCK, 1, LANES), F32),
                     pl.BlockSpec((nblk, 1, LANES), lambda t: (t, 0, 0))))
    return pl.pallas_call(
        _in_proj_kernel,
        grid=(n // tm,),
        in_specs=[pl.BlockSpec((tm, D_MODEL), row),
                  pl.BlockSpec(w_proj.shape, const),
                  pl.BlockSpec(w_q2.shape, const),
                  pl.BlockSpec((1, MLA_Q_RANK), const),
                  pl.BlockSpec((1, MLA_KV_RANK), const),
                  pl.BlockSpec((6, tm, LANES), lambda t: (0, t % n_pos_tiles, 0))],
        out_specs=[o[1] for o in outs],
        out_shape=[o[0] for o in outs],
        compiler_params=pltpu.CompilerParams(dimension_semantics=("parallel",), vmem_limit_bytes=VMEM_LIMIT_BYTES),
        name="in_proj",
    )(x2d, w_proj, w_q2, q_norm, kv_norm, tabs)


def _top3_bias(gate, n_valid, own):
    lane = lax.broadcasted_iota(jnp.int32, gate.shape, 1).astype(F32)
    g = jnp.where(lane < n_valid, gate, NEG)
    bias = jnp.where(lane == own, 0.0, NEG)
    for _ in range(MOBA_TOPK):
        mx = jnp.max(g, axis=-1, keepdims=True)
        idx = jnp.min(jnp.where(g == mx, lane, float(LANES)), axis=-1, keepdims=True)
        hit = lane == idx
        bias = jnp.where(jnp.logical_and(hit, mx > 0.5 * NEG), 0.0, bias)
        g = jnp.where(hit, NEG, g)
    return bias


def _diff_lambda(lam_ref, lam_init):
    a = jnp.sum(lam_ref[0:1, :] * lam_ref[1:2, :], axis=-1, keepdims=True)
    b = jnp.sum(lam_ref[2:3, :] * lam_ref[3:4, :], axis=-1, keepdims=True)
    return jnp.exp(a) - jnp.exp(b) + lam_init


def _diff_combine(o0, o1, lam, subln, lam_init):
    o = o0 - lam * o1
    o = o * lax.rsqrt(jnp.mean(o * o, axis=-1, keepdims=True) + DIFF_SUBLN_EPS) * subln
    return o * (1.0 - lam_init)


def _flash_kernel(*refs, mode, dq, lam_init):
    if mode == "moba":
        q_ref, k_ref, v_ref, kmean_ref, o_ref, m_scr, l_scr, acc_scr, qaug_scr = refs
    elif mode == "diff":
        q_ref, k_ref, v_ref, lam_ref, subln_ref, o_ref, m_scr, l_scr, acc_scr = refs
    else:
        q_ref, k_ref, v_ref, o_ref, m_scr, l_scr, acc_scr = refs
    tq = q_ref.shape[1]
    tk = k_ref.shape[1]
    i = pl.program_id(1)
    j = pl.program_id(2)

    @pl.when(j == 0)
    def _init():
        m_scr[...] = jnp.full(m_scr.shape, NEG, F32)
        l_scr[...] = jnp.zeros(l_scr.shape, F32)
        acc_scr[...] = jnp.zeros(acc_scr.shape, F32)
        if mode == "moba":
            km = kmean_ref[0]
            for s in range(N_STREAMS):
                qs = q_ref[0, :, s * dq:(s + 1) * dq]
                bias = _top3_bias(_nt_dot(qs, km), i.astype(F32), i.astype(F32))
                qaug_scr[s, :, 0:dq] = qs
                qaug_scr[s, :, dq:2 * dq] = bias.astype(BF16)

    def step(diagonal):
        k = k_ref[0]
        v = v_ref[0]
        if mode == "moba":
            onehot = jnp.where(lax.broadcasted_iota(jnp.int32, (tk, LANES), 1) == j, 1.0, 0.0).astype(BF16)
            k = jnp.concatenate([k, onehot], axis=1)
        if diagonal:
            causal = (lax.broadcasted_iota(jnp.int32, (tq, tk), 1)
                      <= lax.broadcasted_iota(jnp.int32, (tq, tk), 0))
        for s in range(N_STREAMS):
            if mode == "moba":
                qs = qaug_scr[s]
            else:
                qs = q_ref[0, :, s * dq:(s + 1) * dq]
            sc = _nt_dot(qs, k)
            if diagonal:
                sc = jnp.where(causal, sc, NEG)
            m_prev = m_scr[s]
            m_new = jnp.maximum(m_prev, jnp.max(sc, axis=-1, keepdims=True))
            alpha = jnp.exp(m_prev - m_new)
            p = jnp.exp(sc - m_new)
            l_scr[s] = alpha * l_scr[s] + jnp.sum(p, axis=-1, keepdims=True)
            acc_scr[s] = alpha * acc_scr[s] + _dot(p.astype(BF16), v)
            m_scr[s] = m_new

    @pl.when(j < i)
    def _past():
        step(False)

    @pl.when(j == i)
    def _diag():
        step(True)
        outs = [acc_scr[s] / l_scr[s] for s in range(N_STREAMS)]
        if mode == "mla":
            o_ref[0] = jnp.concatenate(outs, axis=1).astype(o_ref.dtype)
        elif mode == "moba":
            lo = lax.broadcasted_iota(jnp.int32, (tq, LANES), 1) < (LANES // 2)
            o_ref[0] = jnp.concatenate(
                [jnp.where(lo, outs[2 * v], outs[2 * v + 1]) for v in range(N_STREAMS // 2)],
                axis=1).astype(o_ref.dtype)
        else:
            lam = _diff_lambda(lam_ref, lam_init)
            o_ref[0] = jnp.concatenate(
                [_diff_combine(outs[2 * h], outs[2 * h + 1], lam, subln_ref[...], lam_init)
                 for h in range(DIFF_HEADS)], axis=1).astype(o_ref.dtype)


def _flash(q, k, v, extras, *, mode, tq, lam_init=0.0):
    b, s, qw = q.shape
    dq = qw // N_STREAMS
    dk = k.shape[2]
    out_w = N_STREAMS * LANES if mode == "mla" else (N_STREAMS // 2) * LANES
    kv_map = lambda bb, i, j: (bb, jnp.minimum(i, j), 0)
    in_specs = [pl.BlockSpec((1, tq, qw), lambda bb, i, j: (bb, i, 0)),
                pl.BlockSpec((1, tq, dk), kv_map),
                pl.BlockSpec((1, tq, LANES), kv_map)]
    scratch = [pltpu.VMEM((N_STREAMS, tq, 1), F32), pltpu.VMEM((N_STREAMS, tq, 1), F32),
               pltpu.VMEM((N_STREAMS, tq, LANES), F32)]
    if mode == "moba":
        assert tq == MOBA_BLOCK and s // MOBA_BLOCK <= LANES
        in_specs.append(pl.BlockSpec((1, LANES, LANES), lambda bb, i, j: (bb, 0, 0)))
        scratch.append(pltpu.VMEM((N_STREAMS, tq, 2 * dq), BF16))
    elif mode == "diff":
        in_specs.append(pl.BlockSpec((8, DIFF_HEAD_DIM), lambda bb, i, j: (0, 0)))
        in_specs.append(pl.BlockSpec((1, LANES), lambda bb, i, j: (0, 0)))
    return pl.pallas_call(
        functools.partial(_flash_kernel, mode=mode, dq=dq, lam_init=lam_init),
        grid=(b, s // tq, s // tq),
        in_specs=in_specs,
        out_specs=pl.BlockSpec((1, tq, out_w), lambda bb, i, j: (bb, i, 0)),
        out_shape=jax.ShapeDtypeStruct((b, s, out_w), BF16),
        scratch_shapes=scratch,
        compiler_params=pltpu.CompilerParams(dimension_semantics=("parallel", "parallel", "arbitrary"),
                                             vmem_limit_bytes=VMEM_LIMIT_BYTES),
        name="flash_" + mode,
    )(q, k, v, *extras)


def _decode_kernel(*refs, mode, layer, n_pages, chunk, lam_init):
    pt_ref = refs[0]
    if mode == "diff":
        q_ref, knew_ref, vnew_ref, lam_ref, subln_ref, ca_hbm, cb_hbm, o_ref, buf_a, buf_b, sems = refs[1:]
    else:
        q_ref, knew_ref, vnew_ref, ca_hbm, cb_hbm, o_ref, buf_a, buf_b, sems = refs[1:]
    a_transposed = mode in ("moba", "diff")
    b_transposed = mode in ("moba", "mla")
    n_seq = pl.num_programs(0)
    sq = pl.program_id(0)
    slot = sq % 2
    seq_len = n_pages * PAGE_SIZE

    def page_copy(src, dst, transposed, seq, p, sl, which):
        pg = pt_ref[seq, p]
        off = pl.multiple_of(p * PAGE_SIZE, PAGE_SIZE)
        if transposed:
            dst_view = dst.at[sl, :, pl.ds(off, PAGE_SIZE)]
        else:
            dst_view = dst.at[sl, pl.ds(off, PAGE_SIZE), :]
        return pltpu.make_async_copy(src.at[layer, pg], dst_view, sems.at[sl, which])

    def start_seq(seq, sl):
        def body(p, c):
            page_copy(ca_hbm, buf_a, a_transposed, seq, p, sl, 0).start()
            page_copy(cb_hbm, buf_b, b_transposed, seq, p, sl, 1).start()
            return c
        lax.fori_loop(0, n_pages, body, 0)

    def wait_seq(seq, sl):
        def body(p, c):
            page_copy(ca_hbm, buf_a, a_transposed, seq, p, sl, 0).wait()
            page_copy(cb_hbm, buf_b, b_transposed, seq, p, sl, 1).wait()
            return c
        lax.fori_loop(0, n_pages, body, 0)

    @pl.when(sq == 0)
    def _prime():
        start_seq(sq, slot)

    @pl.when(sq + 1 < n_seq)
    def _prefetch():
        start_seq(sq + 1, 1 - slot)

    wait_seq(sq, slot)

    n_chunks = seq_len // chunk
    q = q_ref[0]
    qf = q.astype(F32)
    knew = knew_ref[0]
    vnew = vnew_ref[0]
    s_new = jnp.sum(qf * knew, axis=-1, keepdims=True)

    def a_chunk(c):
        if a_transposed:
            return buf_a[slot, :, c * chunk:(c + 1) * chunk].astype(BF16)
        return buf_a[slot, c * chunk:(c + 1) * chunk, :].astype(BF16)

    def b_chunk(c):
        if b_transposed:
            return buf_b[slot, :, c * chunk:(c + 1) * chunk].astype(BF16)
        return buf_b[slot, c * chunk:(c + 1) * chunk, :].astype(BF16)

    if mode == "mla":
        q_lat = q[:, 0:MLA_KV_RANK]
        q_rot = q[:, MLA_KV_RANK:MLA_KV_RANK + MLA_ROPE_DIM]
        sc = jnp.concatenate([_nt_dot(q_lat, a_chunk(c)) + _dot(q_rot, b_chunk(c)) for c in range(n_chunks)], axis=1)
    else:
        sc = jnp.concatenate([_dot(q, a_chunk(c)) for c in range(n_chunks)], axis=1)

    if mode == "moba":
        n_blocks = seq_len // MOBA_BLOCK
        lane = lax.broadcasted_iota(jnp.int32, (N_STREAMS, LANES), 1)
        gate = jnp.zeros((N_STREAMS, LANES), F32)
        for nb in range(n_blocks):
            tot = jnp.sum(sc[:, nb * MOBA_BLOCK:(nb + 1) * MOBA_BLOCK], axis=-1, keepdims=True)
            gate = jnp.where(lane == nb, tot, gate)
        bias = _top3_bias(gate, float(n_blocks), float(LANES))
        sc = jnp.concatenate(
            [sc[:, nb * MOBA_BLOCK:(nb + 1) * MOBA_BLOCK] + bias[:, nb:nb + 1] for nb in range(n_blocks)], axis=1)

    m = jnp.maximum(jnp.max(sc, axis=-1, keepdims=True), s_new)
    p = jnp.exp(sc - m)
    p_new = jnp.exp(s_new - m)
    l = jnp.sum(p, axis=-1, keepdims=True) + p_new
    pb = p.astype(BF16)
    acc = p_new * vnew
    for c in range(n_chunks):
        pc = pb[:, c * chunk:(c + 1) * chunk]
        if mode == "moba":
            acc = acc + _nt_dot(pc, b_chunk(c))
        elif mode == "mla":
            acc = acc + _dot(pc, a_chunk(c))
        else:
            acc = acc + _dot(pc, b_chunk(c))
    o = acc / l

    if mode == "mla":
        o_ref[0] = o.astype(o_ref.dtype)
    elif mode == "moba":
        lo = lax.broadcasted_iota(jnp.int32, (1, LANES), 1) < (LANES // 2)
        o_ref[0] = jnp.concatenate(
            [jnp.where(lo, o[2 * v:2 * v + 1], o[2 * v + 1:2 * v + 2]) for v in range(N_STREAMS // 2)],
            axis=0).astype(o_ref.dtype)
    else:
        lam = _diff_lambda(lam_ref, lam_init)
        o_ref[0] = jnp.concatenate(
            [_diff_combine(o[2 * h:2 * h + 1], o[2 * h + 1:2 * h + 2], lam, subln_ref[...], lam_init)
             for h in range(DIFF_HEADS)], axis=0).astype(o_ref.dtype)


def _decode(page_table, q, knew, vnew, extras, cache_a, cache_b, *, mode, layer, lam_init=0.0):
    n_seq, n_pages = page_table.shape
    seq_len = n_pages * PAGE_SIZE
    assert seq_len % MOBA_BLOCK == 0 and seq_len // MOBA_BLOCK < LANES
    dq = q.shape[2]
    out_rows = N_STREAMS if mode == "mla" else N_STREAMS // 2
    chunk = min(seq_len, 2048)

    def buf_shape(cache, transposed):
        feat = cache.shape[2] if transposed else cache.shape[3]
        return (2, feat, seq_len) if transposed else (2, seq_len, feat)

    in_specs = [pl.BlockSpec((1, N_STREAMS, dq), lambda s, pt: (s, 0, 0)),
                pl.BlockSpec((1, 1, dq), lambda s, pt: (s, 0, 0)),
                pl.BlockSpec((1, 1, LANES), lambda s, pt: (s, 0, 0))]
    if mode == "diff":
        in_specs.append(pl.BlockSpec((8, DIFF_HEAD_DIM), lambda s, pt: (0, 0)))
        in_specs.append(pl.BlockSpec((1, LANES), lambda s, pt: (0, 0)))
    in_specs += [pl.BlockSpec(memory_space=pl.ANY), pl.BlockSpec(memory_space=pl.ANY)]
    return pl.pallas_call(
        functools.partial(_decode_kernel, mode=mode, layer=layer, n_pages=n_pages, chunk=chunk, lam_init=lam_init),
        grid_spec=pltpu.PrefetchScalarGridSpec(
            num_scalar_prefetch=1,
            grid=(n_seq,),
            in_specs=in_specs,
            out_specs=pl.BlockSpec((1, out_rows, LANES), lambda s, pt: (s, 0, 0)),
            scratch_shapes=[pltpu.VMEM(buf_shape(cache_a, mode in ("moba", "diff")), F32),
                            pltpu.VMEM(buf_shape(cache_b, mode in ("moba", "mla")), F32),
                            pltpu.SemaphoreType.DMA((2, 2))],
        ),
        out_shape=jax.ShapeDtypeStruct((n_seq, out_rows, LANES), BF16),
        compiler_params=pltpu.CompilerParams(dimension_semantics=("arbitrary",), vmem_limit_bytes=VMEM_LIMIT_BYTES),
        name="decode_" + mode,
    )(page_table, q, knew, vnew, *extras, cache_a, cache_b)


def _merge_kernel(x_ref, oa_ref, ob_ref, oc_ref, wgt_ref, wa_ref, wb_ref, wc_ref, wo_ref, g_ref, b_ref, o_ref, *, alpha):
    x = x_ref[...]
    xb = x.astype(BF16)
    mix = None
    for br, (br_ref, w_ref) in enumerate(((oa_ref, wa_ref), (ob_ref, wb_ref), (oc_ref, wc_ref))):
        gate = jax.nn.sigmoid(_dot(xb, wgt_ref[:, br * D_MODEL:(br + 1) * D_MODEL]))
        y = gate * _dot(br_ref[...], w_ref[...])
        mix = y if mix is None else mix + y
    r = alpha * x + _dot(mix.astype(BF16), wo_ref[...])
    o_ref[...] = _layer_norm(r, g_ref[...], b_ref[...])


def _merge(x2d, oa, ob, oc, w_gt, w_a, w_b, w_c, w_o, g, b, *, tm, alpha):
    n = x2d.shape[0]
    row = lambda t: (t, 0)
    const = lambda t: (0, 0)
    full = lambda a: pl.BlockSpec(a.shape, const)
    return pl.pallas_call(
        functools.partial(_merge_kernel, alpha=alpha),
        grid=(n // tm,),
        in_specs=[pl.BlockSpec((tm, D_MODEL), row), pl.BlockSpec((tm, oa.shape[1]), row),
                  pl.BlockSpec((tm, ob.shape[1]), row), pl.BlockSpec((tm, oc.shape[1]), row),
                  full(w_gt), full(w_a), full(w_b), full(w_c), full(w_o), full(g), full(b)],
        out_specs=pl.BlockSpec((tm, D_MODEL), row),
        out_shape=jax.ShapeDtypeStruct((n, D_MODEL), F32),
        compiler_params=pltpu.CompilerParams(dimension_semantics=("parallel",), vmem_limit_bytes=VMEM_LIMIT_BYTES),
        name="merge",
    )(x2d, oa, ob, oc, w_gt, w_a, w_b, w_c, w_o, g, b)


def _router_gate(x, router):
    x_hi = x.astype(BF16)
    x_lo = (x - x_hi.astype(F32)).astype(BF16)
    r_hi = router.astype(BF16)
    r_lo = (router - r_hi.astype(F32)).astype(BF16)
    logits = _dot(x_hi, r_hi) + (_dot(x_hi, r_lo) + _dot(x_lo, r_hi))
    lane = lax.broadcasted_iota(jnp.int32, logits.shape, 1).astype(F32)
    lg = jnp.where(lane < N_EXPERTS, logits, NEG)
    m1 = jnp.max(lg, axis=-1, keepdims=True)
    i1 = jnp.min(jnp.where(lg == m1, lane, float(LANES)), axis=-1, keepdims=True)
    lg2 = jnp.where(lane == i1, NEG, lg)
    m2 = jnp.max(lg2, axis=-1, keepdims=True)
    i2 = jnp.min(jnp.where(lg2 == m2, lane, float(LANES)), axis=-1, keepdims=True)
    e2 = jnp.exp(m2 - m1)
    den = 1.0 + e2
    return jnp.where(lane == i1, 1.0 / den, 0.0) + jnp.where(lane == i2, e2 / den, 0.0)


def _ffn_kernel(*refs, alpha, moe):
    if moe:
        x_ref, router_ref, wg_ref, wu_ref, wd_ref, g_ref, b_ref, o_ref, acc_scr, xb_scr, gate_scr = refs
    else:
        x_ref, wg_ref, wu_ref, wd_ref, g_ref, b_ref, o_ref, acc_scr, xb_scr = refs
    e = pl.program_id(1)
    c = pl.program_id(2)
    first = jnp.logical_and(e == 0, c == 0)
    last = jnp.logical_and(e == pl.num_programs(1) - 1, c == pl.num_programs(2) - 1)

    @pl.when(first)
    def _init():
        x = x_ref[...]
        acc_scr[...] = jnp.zeros(acc_scr.shape, F32)
        xb_scr[...] = x.astype(BF16)
        if moe:
            gate_scr[...] = _router_gate(x, router_ref[...])

    xb = xb_scr[...]
    h = jax.nn.silu(_dot(xb, wg_ref[0])) * _dot(xb, wu_ref[0])
    if moe:
        gate = gate_scr[...]
        lane = lax.broadcasted_iota(jnp.int32, gate.shape, 1)
        h = h * jnp.sum(jnp.where(lane == e, gate, 0.0), axis=-1, keepdims=True)
    acc_scr[...] += _dot(h.astype(BF16), wd_ref[0])

    @pl.when(last)
    def _fin():
        o_ref[...] = _layer_norm(alpha * x_ref[...] + acc_scr[...], g_ref[...], b_ref[...])


def _ffn(x2d, router, wg, wu, wd, g, b, *, tm, fc, alpha):
    n = x2d.shape[0]
    n_e, _, f = wg.shape
    moe = router is not None
    row = lambda t, e, c: (t, 0)
    const = lambda t, e, c: (0, 0)
    in_specs = [pl.BlockSpec((tm, D_MODEL), row)]
    args = [x2d]
    if moe:
        in_specs.append(pl.BlockSpec(router.shape, const))
        args.append(router)
    in_specs += [pl.BlockSpec((1, D_MODEL, fc), lambda t, e, c: (e, 0, c)),
                 pl.BlockSpec((1, D_MODEL, fc), lambda t, e, c: (e, 0, c)),
                 pl.BlockSpec((1, fc, D_MODEL), lambda t, e, c: (e, c, 0)),
                 pl.BlockSpec(g.shape, const), pl.BlockSpec(b.shape, const)]
    args += [wg, wu, wd, g, b]
    scratch = [pltpu.VMEM((tm, D_MODEL), F32), pltpu.VMEM((tm, D_MODEL), BF16)]
    if moe:
        scratch.append(pltpu.VMEM((tm, LANES), F32))
    return pl.pallas_call(
        functools.partial(_ffn_kernel, alpha=alpha, moe=moe),
        grid=(n // tm, n_e, f // fc),
        in_specs=in_specs,
        out_specs=pl.BlockSpec((tm, D_MODEL), row),
        out_shape=jax.ShapeDtypeStruct((n, D_MODEL), F32),
        scratch_shapes=scratch,
        compiler_params=pltpu.CompilerParams(dimension_semantics=("parallel", "arbitrary", "arbitrary"),
                                             vmem_limit_bytes=VMEM_LIMIT_BYTES),
        name="moe_ffn" if moe else "ffn",
    )(*args)


def _rope_tables(pos):
    posf = pos.astype(F32)[:, None]

    def angles(dim):
        inv = ROPE_THETA ** (-jnp.arange(0, dim, 2, dtype=F32) / dim)
        ang = posf * inv[None, :]
        return jnp.cos(ang), jnp.sin(ang)

    lane = jnp.arange(LANES)
    cos_a, sin_a = angles(MOBA_ROPE_DIM)
    ha = MOBA_ROPE_DIM // 2
    ia = (lane % MOBA_HEAD_DIM)[None, :]
    ca = jnp.where(ia < 2 * ha, cos_a[:, lane % ha], 1.0)
    s1a = jnp.where(ia < ha, -sin_a[:, lane % ha], 0.0)
    s2a = jnp.where((ia >= ha) & (ia < 2 * ha), sin_a[:, lane % ha], 0.0)
    cos_b, sin_b = angles(MLA_ROPE_DIM)
    hb = MLA_ROPE_DIM // 2
    ib = (lane % MLA_ROPE_DIM)[None, :]
    cb = cos_b[:, lane % hb]
    s1b = jnp.where(ib < hb, -sin_b[:, lane % hb], 0.0)
    s2b = jnp.where(ib >= hb, sin_b[:, lane % hb], 0.0)
    return jnp.stack([ca, s1a, s2a, cb, s1b, s2b]).astype(F32)


def _pack_layer_weights(w_in_l, w_uq_l, w_uk_l, w_uv_l, w_br_moba_l, w_br_mla_l):
    pts, acc = [], 0
    for w in IN_WIDTHS[:-1]:
        acc += w
        pts.append(acc)
    mq, mk, mv, cq, ckv, kr, dq, dk, dv, gt = jnp.split(w_in_l, pts, axis=1)
    order = jnp.array(MOBA_HEAD_ORDER)
    mq = mq.reshape(D_MODEL, MOBA_HEADS, MOBA_HEAD_DIM)[:, order, :].reshape(D_MODEL, -1) * MOBA_SCALE
    kr = jnp.pad(kr, ((0, 0), (0, LANES - MLA_ROPE_DIM)))
    w_proj = jnp.concatenate([mq, mk, mv, cq, ckv, kr, dq * DIFF_SCALE, dk, dv], axis=1).astype(BF16)
    w_gt = gt.astype(BF16)

    qd = MLA_NOPE_DIM + MLA_ROPE_DIM
    uq = w_uq_l.reshape(MLA_Q_RANK, MLA_HEADS, qd)
    uq_nope = jnp.transpose(uq[:, :, :MLA_NOPE_DIM], (1, 0, 2))
    uk_t = jnp.transpose(w_uk_l, (1, 2, 0))
    w_abs = _fold(uq_nope, uk_t)
    uq_rope = jnp.transpose(uq[:, :, MLA_NOPE_DIM:], (1, 0, 2))
    uq_rope = jnp.pad(uq_rope, ((0, 0), (0, 0), (0, LANES - MLA_ROPE_DIM)))
    w_q2 = jnp.concatenate([w_abs, uq_rope], axis=2) * MLA_SCALE
    w_q2 = jnp.transpose(w_q2, (1, 0, 2)).reshape(MLA_Q_RANK, MLA_HEADS * 2 * LANES).astype(BF16)

    uv = jnp.transpose(w_uv_l, (1, 0, 2))
    br = w_br_mla_l.reshape(MLA_HEADS, MLA_V_DIM, D_MODEL)
    w_ob = _fold(uv, br).reshape(MLA_HEADS * MLA_KV_RANK, D_MODEL).astype(BF16)

    w_a = w_br_moba_l.reshape(MOBA_HEADS, MOBA_HEAD_DIM, D_MODEL)[order].reshape(-1, D_MODEL).astype(BF16)
    return w_proj, w_gt, w_q2, w_ob, w_a


def _pick_tile(n, pref):
    t = min(n, pref)
    while n % t:
        t //= 2
    return t


def _pick_ff_chunk(f, cap=1536):
    best = LANES
    for c in range(LANES, min(f, cap) + 1, LANES):
        if f % c == 0:
            best = c
    return best


def kernel(x_prompt, x_sample, cache_moba_k, cache_moba_v, cache_mla_ckv, cache_mla_krope, cache_diff_k, cache_diff_v, page_table, w_in, mla_q_norm, mla_w_uq, mla_kv_norm, mla_w_uk, mla_w_uv, diff_lambda_q1, diff_lambda_k1, diff_lambda_q2, diff_lambda_k2, diff_subln, w_br_moba, w_br_mla, w_br_diff, w_o, ln1_g, ln1_b, ln2_g, ln2_b, ffn_w_gate, ffn_w_up, ffn_w_down, moe_router, moe_w_gate, moe_w_up, moe_w_down):
    bsz, seq, _ = x_prompt.shape
    n_seq, dec_t, _ = x_sample.shape
    assert dec_t == 1 and seq % MOBA_BLOCK == 0
    depth = w_in.shape[0]
    n_pool = cache_moba_k.shape[1]
    n_pages = page_table.shape[1]
    past = n_pages * PAGE_SIZE
    alpha = (2 * depth) ** 0.25
    n_p = bsz * seq

    kt_moba = jnp.transpose(cache_moba_k, (0, 1, 3, 4, 2)).reshape(depth, n_pool, LANES, PAGE_SIZE)
    vt_moba = jnp.transpose(cache_moba_v, (0, 1, 3, 4, 2)).reshape(depth, n_pool, LANES, PAGE_SIZE)
    krt_mla = jnp.transpose(cache_mla_krope, (0, 1, 3, 2))
    kt_diff = jnp.transpose(cache_diff_k, (0, 1, 3, 4, 5, 2)).reshape(depth, n_pool, LANES, PAGE_SIZE)
    v_diff = cache_diff_v.reshape(depth, n_pool, PAGE_SIZE, LANES)

    tabs_p = _rope_tables(jnp.arange(seq, dtype=jnp.int32))
    tabs_s = jnp.broadcast_to(_rope_tables(past + jnp.arange(1, dtype=jnp.int32)), (6, n_seq, LANES))

    tm_p = _pick_tile(seq, 512)
    tq_big = _pick_tile(seq, 512)
    tm_ffn = _pick_tile(n_p, 512)

    xp = x_prompt.reshape(n_p, D_MODEL)
    xs = x_sample.reshape(n_seq, D_MODEL)
    leaves = [[] for _ in range(12)]

    for l in range(depth):
        lam_init = 0.8 - 0.6 * math.exp(-0.3 * l)
        w_proj, w_gt, w_q2, w_ob, w_a = _pack_layer_weights(
            w_in[l], mla_w_uq[l], mla_w_uk[l], mla_w_uv[l], w_br_moba[l], w_br_mla[l])
        w_c = w_br_diff[l].astype(BF16)
        w_out = w_o[l].astype(BF16)
        qn = mla_q_norm[l][None, :]
        kvn = mla_kv_norm[l][None, :]
        lam_rows = jnp.concatenate([jnp.stack([diff_lambda_q1[l], diff_lambda_k1[l], diff_lambda_q2[l], diff_lambda_k2[l]]),
                                    jnp.zeros((4, DIFF_HEAD_DIM), F32)]).astype(F32)
        subln = diff_subln[l][None, :]

        (mq, mk, mv, mkb, mvb, qcat, ckv, kr, kcat, dq, dk, dv, dkb, dvb, kmean) = _in_proj(
            xp, w_proj, w_q2, qn, kvn, tabs_p, tm=tm_p, with_kmean=True)
        nb = seq // MOBA_BLOCK
        kmean_b = jnp.pad(kmean.reshape(bsz, nb, LANES), ((0, 0), (0, LANES - nb), (0, 0))).astype(BF16)
        r3 = lambda a: a.reshape(bsz, seq, a.shape[1])
        oa_p = _flash(r3(mq), r3(mkb), r3(mvb), (kmean_b,), mode="moba", tq=MOBA_BLOCK)
        ob_p = _flash(r3(qcat), r3(kcat), r3(kcat), (), mode="mla", tq=tq_big)
        oc_p = _flash(r3(dq), r3(dkb), r3(dvb), (lam_rows, subln), mode="diff", tq=tq_big, lam_init=lam_init)

        (mq_s, mk_s, mv_s, _, _, qcat_s, ckv_s, kr_s, _, dq_s, dk_s, dv_s, _, _) = _in_proj(
            xs, w_proj, w_q2, qn, kvn, tabs_s, tm=n_seq, with_kmean=False)
        q3 = lambda a: a.reshape(n_seq, N_STREAMS, a.shape[1] // N_STREAMS)
        r1 = lambda a: a[:, None, :]
        kcat_s = jnp.concatenate([ckv_s, kr_s, jnp.zeros((n_seq, LANES - MLA_ROPE_DIM), F32)], axis=1)
        oa_s = _decode(page_table, q3(mq_s), r1(mk_s), r1(mv_s), (), kt_moba, vt_moba, mode="moba", layer=l)
        ob_s = _decode(page_table, q3(qcat_s), r1(kcat_s), r1(ckv_s), (), cache_mla_ckv, krt_mla, mode="mla", layer=l)
        oc_s = _decode(page_table, q3(dq_s), r1(dk_s), r1(dv_s), (lam_rows, subln), kt_diff, v_diff,
                       mode="diff", layer=l, lam_init=lam_init)

        g1, b1 = ln1_g[l][None, :], ln1_b[l][None, :]
        g2, b2 = ln2_g[l][None, :], ln2_b[l][None, :]
        xp = _merge(xp, oa_p.reshape(n_p, -1), ob_p.reshape(n_p, -1), oc_p.reshape(n_p, -1),
                    w_gt, w_a, w_ob, w_c, w_out, g1, b1, tm=tm_p, alpha=alpha)
        xs = _merge(xs, oa_s.reshape(n_seq, -1), ob_s.reshape(n_seq, -1), oc_s.reshape(n_seq, -1),
                    w_gt, w_a, w_ob, w_c, w_out, g1, b1, tm=n_seq, alpha=alpha)
        j = l // 2
        if l % 2 == 0:
            router = None
            wg, wu, wd = (ffn_w_gate[j][None].astype(BF16), ffn_w_up[j][None].astype(BF16),
                          ffn_w_down[j][None].astype(BF16))
        else:
            router = jnp.pad(moe_router[j], ((0, 0), (0, LANES - N_EXPERTS)))
            wg, wu, wd = moe_w_gate[j].astype(BF16), moe_w_up[j].astype(BF16), moe_w_down[j].astype(BF16)
        fc = _pick_ff_chunk(wg.shape[2])
        xp = _ffn(xp, router, wg, wu, wd, g2, b2, tm=tm_ffn, fc=fc, alpha=alpha)
        xs = _ffn(xs, router, wg, wu, wd, g2, b2, tm=n_seq, fc=fc, alpha=alpha)

        p5 = lambda a, *tail: a.reshape(bsz, seq, *tail)
        s5 = lambda a, *tail: a.reshape(n_seq, 1, *tail)
        new = [p5(mk, MOBA_KV_HEADS, MOBA_HEAD_DIM), s5(mk_s, MOBA_KV_HEADS, MOBA_HEAD_DIM),
               p5(mv, MOBA_KV_HEADS, MOBA_HEAD_DIM), s5(mv_s, MOBA_KV_HEADS, MOBA_HEAD_DIM),
               p5(ckv, MLA_KV_RANK), s5(ckv_s, MLA_KV_RANK),
               p5(kr, MLA_ROPE_DIM), s5(kr_s, MLA_ROPE_DIM),
               p5(dk, DIFF_KV_HEADS, 2, DIFF_HEAD_DIM), s5(dk_s, DIFF_KV_HEADS, 2, DIFF_HEAD_DIM),
               p5(dv, DIFF_KV_HEADS, 2 * DIFF_HEAD_DIM), s5(dv_s, DIFF_KV_HEADS, 2 * DIFF_HEAD_DIM)]
        for acc_list, a in zip(leaves, new):
            acc_list.append(a)

    return (xp.reshape(bsz, seq, D_MODEL), xs.reshape(n_seq, 1, D_MODEL)) + tuple(jnp.stack(a) for a in leaves)
```

```python
import functools
import math

import jax
import jax.numpy as jnp
from jax import lax
from jax.experimental import pallas as pl
from jax.experimental.pallas import tpu as pltpu

F32 = jnp.float32
BF16 = jnp.bfloat16

D_MODEL = 1024
PAGE_SIZE = 128
MOBA_HEADS = 8
MOBA_KV_HEADS = 2
MOBA_HEAD_DIM = 64
MOBA_BLOCK = 256
MOBA_TOPK = 3
MOBA_ROPE_DIM = MOBA_HEAD_DIM // 4
MOBA_SCALE = MOBA_HEAD_DIM ** -0.5
MLA_HEADS = 8
MLA_Q_RANK = 256
MLA_KV_RANK = 128
MLA_NOPE_DIM = 64
MLA_ROPE_DIM = 32
MLA_V_DIM = 64
MLA_SCALE = (MLA_NOPE_DIM + MLA_ROPE_DIM) ** -0.5
DIFF_HEADS = 4
DIFF_KV_HEADS = 1
DIFF_HEAD_DIM = 64
DIFF_ROPE_DIM = DIFF_HEAD_DIM // 4
DIFF_SCALE = DIFF_HEAD_DIM ** -0.5
N_BRANCH = 3
ROPE_THETA = 500000.0
LN_EPS = 1e-5
RMS_EPS = 1e-6
DIFF_SUBLN_EPS = 1e-5
N_EXPERTS = 8
IN_WIDTHS = (MOBA_HEADS * MOBA_HEAD_DIM, MOBA_KV_HEADS * MOBA_HEAD_DIM, MOBA_KV_HEADS * MOBA_HEAD_DIM,
             MLA_Q_RANK, MLA_KV_RANK, MLA_ROPE_DIM,
             DIFF_HEADS * 2 * DIFF_HEAD_DIM, DIFF_KV_HEADS * 2 * DIFF_HEAD_DIM, DIFF_KV_HEADS * 2 * DIFF_HEAD_DIM,
             N_BRANCH * D_MODEL)

LANES = 128
VMEM_LIMIT_BYTES = 56 * 1024 * 1024
N_STREAMS = 8
NEG = -1e30
LOG2E = math.log2(math.e)
ONES_ROWS = 16
MOBA_HEAD_ORDER = (0, 4, 1, 5, 2, 6, 3, 7)

_C_MQ, _C_MK, _C_MV, _C_CQ, _C_CKV, _C_KR, _C_DQ, _C_DK, _C_DV, _C_END = (
    0, 512, 640, 768, 1024, 1152, 1280, 1792, 1920, 2048)


def _nt_dot(a, b):
    return lax.dot_general(a, b, (((1,), (1,)), ((), ())), preferred_element_type=F32)


def _dot(a, b):
    return jnp.dot(a, b, preferred_element_type=F32)


def _layer_norm(r, g, b):
    mu = jnp.mean(r, axis=-1, keepdims=True)
    d = r - mu
    var = jnp.mean(d * d, axis=-1, keepdims=True)
    return d * lax.rsqrt(var + LN_EPS) * g + b


def _rope128(h, cos, s1, s2, half):
    return h * cos + pltpu.roll(h, LANES - half, 1) * s1 + pltpu.roll(h, half, 1) * s2


def _fold_kernel(a_ref, b_ref, o_ref):
    o_ref[0] = jnp.dot(a_ref[0], b_ref[0], preferred_element_type=F32, precision=lax.Precision.HIGHEST)


def _fold(a, b):
    nh, m, k = a.shape
    n = b.shape[2]
    return pl.pallas_call(
        _fold_kernel,
        grid=(nh,),
        in_specs=[pl.BlockSpec((1, m, k), lambda h: (h, 0, 0)), pl.BlockSpec((1, k, n), lambda h: (h, 0, 0))],
        out_specs=pl.BlockSpec((1, m, n), lambda h: (h, 0, 0)),
        out_shape=jax.ShapeDtypeStruct((nh, m, n), F32),
        name="fold_weights",
    )(a, b)


def _in_proj_kernel(x_ref, w_ref, wq2_ref, qn_ref, kvn_ref, tab_ref,
                    mq_ref, mk_ref, mv_ref, mkb_ref, mvt_ref, qcat_ref, ckv_ref, kr_ref, kcat_ref, ckvt_ref,
                    dq_ref, dk_ref, dv_ref, dkb_ref, dvt_ref, *kmean_refs):
    tm = x_ref.shape[0]
    xb = x_ref[...].astype(BF16)
    cos_a, s1_a, s2_a = tab_ref[0], tab_ref[1], tab_ref[2]
    cos_b, s1_b, s2_b = tab_ref[3], tab_ref[4], tab_ref[5]
    half_a = MOBA_ROPE_DIM // 2
    half_b = MLA_ROPE_DIM // 2
    lo = lax.broadcasted_iota(jnp.int32, (tm, LANES), 1) < (LANES // 2)

    def proj(c0, c1):
        return _dot(xb, w_ref[:, c0:c1])

    def rope_a(h):
        return _rope128(h, cos_a, s1_a, s2_a, half_a)

    def expand_halves(h, n_slabs):
        pieces = []
        for v in range(n_slabs):
            hv = rope_a(h[:, v * LANES:(v + 1) * LANES])
            pieces.append(jnp.where(lo, hv, 0.0))
            pieces.append(jnp.where(lo, 0.0, hv))
        return jnp.concatenate(pieces, axis=1).astype(BF16)

    mq_ref[...] = expand_halves(proj(_C_MQ, _C_MK), 4)
    mk = rope_a(proj(_C_MK, _C_MV))
    mk_ref[...] = mk
    mkb_ref[...] = mk.astype(BF16)
    mv = proj(_C_MV, _C_CQ)
    mv_ref[...] = mv
    mvt_ref[...] = mv.T.astype(BF16)
    if kmean_refs:
        for blk in range(tm // MOBA_BLOCK):
            kmean_refs[0][blk] = jnp.mean(mk[blk * MOBA_BLOCK:(blk + 1) * MOBA_BLOCK], axis=0, keepdims=True)

    cq = proj(_C_CQ, _C_CKV)
    cqn = cq * lax.rsqrt(jnp.mean(cq * cq, axis=-1, keepdims=True) + RMS_EPS) * qn_ref[...]
    q2 = _dot(cqn.astype(BF16), wq2_ref[...])
    pieces = []
    for h in range(MLA_HEADS):
        pieces.append(q2[:, (2 * h) * LANES:(2 * h + 1) * LANES])
        pieces.append(_rope128(q2[:, (2 * h + 1) * LANES:(2 * h + 2) * LANES], cos_b, s1_b, s2_b, half_b))
    qcat_ref[...] = jnp.concatenate(pieces, axis=1).astype(BF16)
    ckv = proj(_C_CKV, _C_KR)
    ckv = ckv * lax.rsqrt(jnp.mean(ckv * ckv, axis=-1, keepdims=True) + RMS_EPS) * kvn_ref[...]
    ckv_ref[...] = ckv
    kr = _rope128(proj(_C_KR, _C_DQ), cos_b, s1_b, s2_b, half_b)
    kr_ref[...] = kr[:, :MLA_ROPE_DIM]
    kcat_ref[...] = jnp.concatenate([ckv, kr], axis=1).astype(BF16)
    ckvt_ref[...] = ckv.T.astype(BF16)

    dq_ref[...] = expand_halves(proj(_C_DQ, _C_DK), 4)
    dk = rope_a(proj(_C_DK, _C_DV))
    dk_ref[...] = dk
    dkb_ref[...] = dk.astype(BF16)
    dv = proj(_C_DV, _C_END)
    dv_ref[...] = dv
    dvt_ref[...] = dv.T.astype(BF16)


def _in_proj(x2d, w_proj, w_q2, q_norm, kv_norm, tabs, *, tm, with_kmean):
    n = x2d.shape[0]
    n_pos_tiles = tabs.shape[1] // tm
    row = lambda t: (t, 0)
    const = lambda t: (0, 0)

    def rows(width, dtype):
        return jax.ShapeDtypeStruct((n, width), dtype), pl.BlockSpec((tm, width), row)

    transposed = (jax.ShapeDtypeStruct((LANES, n), BF16), pl.BlockSpec((LANES, tm), lambda t: (0, t)))
    outs = [rows(1024, BF16), rows(128, F32), rows(128, F32), rows(128, BF16), transposed,
            rows(2048, BF16), rows(128, F32), rows(MLA_ROPE_DIM, F32), rows(256, BF16), transposed,
            rows(1024, BF16), rows(128, F32), rows(128, F32), rows(128, BF16), transposed]
    if with_kmean:
        nblk = tm // MOBA_BLOCK
        outs.append((jax.ShapeDtypeStruct((n // MOBA_BLOCK, 1, LANES), F32),
                     pl.BlockSpec((nblk, 1, LANES), lambda t: (t, 0, 0))))
    return pl.pallas_call(
        _in_proj_kernel,
        grid=(n // tm,),
        in_specs=[pl.BlockSpec((tm, D_MODEL), row),
                  pl.BlockSpec(w_proj.shape, const),
                  pl.BlockSpec(w_q2.shape, const),
                  pl.BlockSpec((1, MLA_Q_RANK), const),
                  pl.BlockSpec((1, MLA_KV_RANK), const),
                  pl.BlockSpec((6, tm, LANES), lambda t: (0, t % n_pos_tiles, 0))],
        out_specs=[o[1] for o in outs],
        out_shape=[o[0] for o in outs],
        compiler_params=pltpu.CompilerParams(dimension_semantics=("parallel",), vmem_limit_bytes=VMEM_LIMIT_BYTES),
        name="in_proj",
    )(x2d, w_proj, w_q2, q_norm, kv_norm, tabs)


def _top3_bias(gate, n_valid, own):
    lane = lax.broadcasted_iota(jnp.int32, gate.shape, 1).astype(F32)
    g = jnp.where(lane < n_valid, gate, NEG)
    bias = jnp.where(lane == own, 0.0, NEG)
    for _ in range(MOBA_TOPK):
        mx = jnp.max(g, axis=-1, keepdims=True)
        idx = jnp.min(jnp.where(g == mx, lane, float(LANES)), axis=-1, keepdims=True)
        hit = lane == idx
        bias = jnp.where(jnp.logical_and(hit, mx > 0.5 * NEG), 0.0, bias)
        g = jnp.where(hit, NEG, g)
    return bias


def _diff_lambda(lam_ref, lam_init):
    a = jnp.sum(lam_ref[0:1, :] * lam_ref[1:2, :], axis=-1, keepdims=True)
    b = jnp.sum(lam_ref[2:3, :] * lam_ref[3:4, :], axis=-1, keepdims=True)
    return jnp.exp(a) - jnp.exp(b) + lam_init


def _diff_combine(o0, o1, lam, subln, lam_init):
    o = o0 - lam * o1
    o = o * lax.rsqrt(jnp.mean(o * o, axis=-1, keepdims=True) + DIFF_SUBLN_EPS) * subln
    return o * (1.0 - lam_init)


def _flash_kernel(*refs, mode, dq, lam_init):
    i_tab, j_tab = refs[0], refs[1]
    if mode == "moba":
        q_ref, k_ref, vt_ref, kmean_ref, o_ref, m_scr, acc_scr, qaug_scr = refs[2:]
    elif mode == "diff":
        q_ref, k_ref, vt_ref, lam_ref, subln_ref, o_ref, m_scr, acc_scr = refs[2:]
    else:
        q_ref, k_ref, vt_ref, o_ref, m_scr, acc_scr = refs[2:]
    tq = q_ref.shape[1]
    tk = k_ref.shape[1]
    i = i_tab[pl.program_id(1)]
    j = j_tab[pl.program_id(1)]
    blocks_per_tile = tq // MOBA_BLOCK
    block_shift = MOBA_BLOCK.bit_length() - 1

    @pl.when(j == 0)
    def _init():
        m_scr[...] = jnp.full(m_scr.shape, NEG, F32)
        acc_scr[...] = jnp.zeros(acc_scr.shape, F32)
        if mode == "moba":
            km = kmean_ref[0]
            row_block = lax.shift_right_logical(lax.broadcasted_iota(jnp.int32, (tq, LANES), 0), block_shift)
            own = (i * blocks_per_tile + row_block).astype(F32)
            for s in range(N_STREAMS):
                qs = q_ref[0, :, s * dq:(s + 1) * dq]
                bias = _top3_bias(_nt_dot(qs, km), own, own)
                qaug_scr[s, :, 0:dq] = qs
                qaug_scr[s, :, dq:2 * dq] = bias.astype(BF16)

    def step(diagonal):
        k = k_ref[0]
        vt = vt_ref[...]
        if mode == "moba":
            key_block = j * blocks_per_tile + lax.shift_right_logical(
                lax.broadcasted_iota(jnp.int32, (tk, LANES), 0), block_shift)
            onehot = jnp.where(lax.broadcasted_iota(jnp.int32, (tk, LANES), 1) == key_block, 1.0, 0.0).astype(BF16)
            k = jnp.concatenate([k, onehot], axis=1)
        if diagonal:
            causal = (lax.broadcasted_iota(jnp.int32, (tk, tq), 0)
                      <= lax.broadcasted_iota(jnp.int32, (tk, tq), 1))
        vt = jnp.concatenate([vt, jnp.ones((ONES_ROWS, tk), BF16)], axis=0)

        def scores(s):
            if mode == "moba":
                qs = qaug_scr[s]
            else:
                qs = q_ref[0, :, s * dq:(s + 1) * dq]
            st = _nt_dot(k, qs)
            if diagonal:
                st = jnp.where(causal, st, NEG)
            m_prev = m_scr[s]
            m_new = jnp.maximum(m_prev, jnp.max(st, axis=0, keepdims=True))
            return st, m_prev, m_new

        def probs(s, st, m_prev, m_new):
            m_scr[s] = m_new
            return jnp.exp2(m_prev - m_new), jnp.exp2(st - m_new).astype(BF16)

        def accumulate(s, alpha, p):
            acc_scr[s] = alpha * acc_scr[s] + _dot(vt, p)

        sc = {0: scores(0), 1: scores(1)}
        pr = {0: probs(0, *sc.pop(0))}
        for s in range(N_STREAMS):
            if s + 2 < N_STREAMS:
                sc[s + 2] = scores(s + 2)
            if s + 1 < N_STREAMS:
                pr[s + 1] = probs(s + 1, *sc.pop(s + 1))
            accumulate(s, *pr.pop(s))

    @pl.when(j < i)
    def _past():
        step(False)

    @pl.when(j == i)
    def _diag():
        step(True)
        outs_t = [acc_scr[s, 0:LANES, :] / acc_scr[s, LANES:LANES + 1, :] for s in range(N_STREAMS)]
        if mode == "mla":
            o_ref[0] = jnp.concatenate([o.T for o in outs_t], axis=1).astype(o_ref.dtype)
        elif mode == "moba":
            half = LANES // 2
            o_ref[0] = jnp.concatenate(
                [jnp.concatenate([outs_t[2 * v][:half], outs_t[2 * v + 1][half:]], axis=0).T
                 for v in range(N_STREAMS // 2)], axis=1).astype(o_ref.dtype)
        else:
            lam = _diff_lambda(lam_ref, lam_init)
            pieces = []
            for h in range(DIFF_HEADS):
                o = outs_t[2 * h] - lam * outs_t[2 * h + 1]
                o = o * lax.rsqrt(jnp.mean(o * o, axis=0, keepdims=True) + DIFF_SUBLN_EPS)
                pieces.append(o.T * subln_ref[...] * (1.0 - lam_init))
            o_ref[0] = jnp.concatenate(pieces, axis=1).astype(o_ref.dtype)


def _flash(q, k, vt, extras, *, mode, tq, lam_init=0.0):
    b, s, qw = q.shape
    dq = qw // N_STREAMS
    dk = k.shape[2]
    n_t = s // tq
    out_w = N_STREAMS * LANES if mode == "mla" else (N_STREAMS // 2) * LANES
    pairs = [(i, j) for i in range(n_t) for j in range(i + 1)]
    i_tab = jnp.array([p[0] for p in pairs], jnp.int32)
    j_tab = jnp.array([p[1] for p in pairs], jnp.int32)
    const2 = lambda bb, p, it, jt: (0, 0)
    in_specs = [pl.BlockSpec((1, tq, qw), lambda bb, p, it, jt: (bb, it[p], 0)),
                pl.BlockSpec((1, tq, dk), lambda bb, p, it, jt: (bb, jt[p], 0)),
                pl.BlockSpec((LANES, tq), lambda bb, p, it, jt: (0, bb * n_t + jt[p]))]
    scratch = [pltpu.VMEM((N_STREAMS, 1, tq), F32), pltpu.VMEM((N_STREAMS, LANES + ONES_ROWS, tq), F32)]
    if mode == "moba":
        assert tq % MOBA_BLOCK == 0 and s // MOBA_BLOCK <= LANES and MOBA_BLOCK & (MOBA_BLOCK - 1) == 0
        in_specs.append(pl.BlockSpec((1, LANES, LANES), lambda bb, p, it, jt: (bb, 0, 0)))
        scratch.append(pltpu.VMEM((N_STREAMS, tq, 2 * dq), BF16))
    elif mode == "diff":
        in_specs.append(pl.BlockSpec((8, DIFF_HEAD_DIM), const2))
        in_specs.append(pl.BlockSpec((1, LANES), const2))
    return pl.pallas_call(
        functools.partial(_flash_kernel, mode=mode, dq=dq, lam_init=lam_init),
        grid_spec=pltpu.PrefetchScalarGridSpec(
            num_scalar_prefetch=2,
            grid=(b, len(pairs)),
            in_specs=in_specs,
            out_specs=pl.BlockSpec((1, tq, out_w), lambda bb, p, it, jt: (bb, it[p], 0)),
            scratch_shapes=scratch,
        ),
        out_shape=jax.ShapeDtypeStruct((b, s, out_w), BF16),
        compiler_params=pltpu.CompilerParams(dimension_semantics=("parallel", "arbitrary"),
                                             vmem_limit_bytes=VMEM_LIMIT_BYTES),
        name="flash_" + mode,
    )(i_tab, j_tab, q, k, vt, *extras)


def _decode_kernel(*refs, mode, layer, n_pages, chunk, lam_init):
    pt_ref = refs[0]
    if mode == "diff":
        q_ref, knew_ref, vnew_ref, lam_ref, subln_ref, ca_hbm, cb_hbm, o_ref, buf_a, buf_b, sems = refs[1:]
    else:
        q_ref, knew_ref, vnew_ref, ca_hbm, cb_hbm, o_ref, buf_a, buf_b, sems = refs[1:]
    a_transposed = mode in ("moba", "diff")
    b_transposed = mode in ("moba", "mla")
    n_seq = pl.num_programs(0)
    sq = pl.program_id(0)
    slot = sq % 2
    seq_len = n_pages * PAGE_SIZE

    def page_copy(src, dst, transposed, seq, p, sl, which):
        pg = pt_ref[seq, p]
        off = pl.multiple_of(p * PAGE_SIZE, PAGE_SIZE)
        if transposed:
            dst_view = dst.at[sl, :, pl.ds(off, PAGE_SIZE)]
        else:
            dst_view = dst.at[sl, pl.ds(off, PAGE_SIZE), :]
        return pltpu.make_async_copy(src.at[layer, pg], dst_view, sems.at[sl, which])

    def start_seq(seq, sl):
        def body(p, c):
            page_copy(ca_hbm, buf_a, a_transposed, seq, p, sl, 0).start()
            page_copy(cb_hbm, buf_b, b_transposed, seq, p, sl, 1).start()
            return c
        lax.fori_loop(0, n_pages, body, 0)

    def wait_seq(seq, sl):
        def body(p, c):
            page_copy(ca_hbm, buf_a, a_transposed, seq, p, sl, 0).wait()
            page_copy(cb_hbm, buf_b, b_transposed, seq, p, sl, 1).wait()
            return c
        lax.fori_loop(0, n_pages, body, 0)

    @pl.when(sq == 0)
    def _prime():
        start_seq(sq, slot)

    @pl.when(sq + 1 < n_seq)
    def _prefetch():
        start_seq(sq + 1, 1 - slot)

    wait_seq(sq, slot)

    n_chunks = seq_len // chunk
    q = q_ref[0]
    qf = q.astype(F32)
    knew = knew_ref[0]
    vnew = vnew_ref[0]
    s_new = jnp.sum(qf * knew, axis=-1, keepdims=True)

    def a_chunk(c):
        if a_transposed:
            return buf_a[slot, :, c * chunk:(c + 1) * chunk].astype(BF16)
        return buf_a[slot, c * chunk:(c + 1) * chunk, :].astype(BF16)

    def b_chunk(c):
        if b_transposed:
            return buf_b[slot, :, c * chunk:(c + 1) * chunk].astype(BF16)
        return buf_b[slot, c * chunk:(c + 1) * chunk, :].astype(BF16)

    if mode == "mla":
        q_lat = q[:, 0:MLA_KV_RANK]
        q_rot = q[:, MLA_KV_RANK:MLA_KV_RANK + MLA_ROPE_DIM]
        sc = jnp.concatenate([_nt_dot(q_lat, a_chunk(c)) + _dot(q_rot, b_chunk(c)) for c in range(n_chunks)], axis=1)
    else:
        sc = jnp.concatenate([_dot(q, a_chunk(c)) for c in range(n_chunks)], axis=1)

    if mode == "moba":
        n_blocks = seq_len // MOBA_BLOCK
        lane = lax.broadcasted_iota(jnp.int32, (N_STREAMS, LANES), 1)
        gate = jnp.zeros((N_STREAMS, LANES), F32)
        for nb in range(n_blocks):
            tot = jnp.sum(sc[:, nb * MOBA_BLOCK:(nb + 1) * MOBA_BLOCK], axis=-1, keepdims=True)
            gate = jnp.where(lane == nb, tot, gate)
        bias = _top3_bias(gate, float(n_blocks), float(LANES))
        sc = jnp.concatenate(
            [sc[:, nb * MOBA_BLOCK:(nb + 1) * MOBA_BLOCK] + bias[:, nb:nb + 1] for nb in range(n_blocks)], axis=1)

    m = jnp.maximum(jnp.max(sc, axis=-1, keepdims=True), s_new)
    p = jnp.exp2(sc - m)
    p_new = jnp.exp2(s_new - m)
    l = jnp.sum(p, axis=-1, keepdims=True) + p_new
    pb = p.astype(BF16)
    acc = p_new * vnew
    for c in range(n_chunks):
        pc = pb[:, c * chunk:(c + 1) * chunk]
        if mode == "moba":
            acc = acc + _nt_dot(pc, b_chunk(c))
        elif mode == "mla":
            acc = acc + _dot(pc, a_chunk(c))
        else:
            acc = acc + _dot(pc, b_chunk(c))
    o = acc / l

    if mode == "mla":
        o_ref[0] = o.astype(o_ref.dtype)
    elif mode == "moba":
        lo = lax.broadcasted_iota(jnp.int32, (1, LANES), 1) < (LANES // 2)
        o_ref[0] = jnp.concatenate(
            [jnp.where(lo, o[2 * v:2 * v + 1], o[2 * v + 1:2 * v + 2]) for v in range(N_STREAMS // 2)],
            axis=0).astype(o_ref.dtype)
    else:
        lam = _diff_lambda(lam_ref, lam_init)
        o_ref[0] = jnp.concatenate(
            [_diff_combine(o[2 * h:2 * h + 1], o[2 * h + 1:2 * h + 2], lam, subln_ref[...], lam_init)
             for h in range(DIFF_HEADS)], axis=0).astype(o_ref.dtype)


def _decode(page_table, q, knew, vnew, extras, cache_a, cache_b, *, mode, layer, lam_init=0.0):
    n_seq, n_pages = page_table.shape
    seq_len = n_pages * PAGE_SIZE
    assert seq_len % MOBA_BLOCK == 0 and seq_len // MOBA_BLOCK < LANES
    dq = q.shape[2]
    out_rows = N_STREAMS if mode == "mla" else N_STREAMS // 2
    chunk = min(seq_len, 2048)

    def buf_shape(cache, transposed):
        feat = cache.shape[2] if transposed else cache.shape[3]
        return (2, feat, seq_len) if transposed else (2, seq_len, feat)

    in_specs = [pl.BlockSpec((1, N_STREAMS, dq), lambda s, pt: (s, 0, 0)),
                pl.BlockSpec((1, 1, dq), lambda s, pt: (s, 0, 0)),
                pl.BlockSpec((1, 1, LANES), lambda s, pt: (s, 0, 0))]
    if mode == "diff":
        in_specs.append(pl.BlockSpec((8, DIFF_HEAD_DIM), lambda s, pt: (0, 0)))
        in_specs.append(pl.BlockSpec((1, LANES), lambda s, pt: (0, 0)))
    in_specs += [pl.BlockSpec(memory_space=pl.ANY), pl.BlockSpec(memory_space=pl.ANY)]
    return pl.pallas_call(
        functools.partial(_decode_kernel, mode=mode, layer=layer, n_pages=n_pages, chunk=chunk, lam_init=lam_init),
        grid_spec=pltpu.PrefetchScalarGridSpec(
            num_scalar_prefetch=1,
            grid=(n_seq,),
            in_specs=in_specs,
            out_specs=pl.BlockSpec((1, out_rows, LANES), lambda s, pt: (s, 0, 0)),
            scratch_shapes=[pltpu.VMEM(buf_shape(cache_a, mode in ("moba", "diff")), F32),
                            pltpu.VMEM(buf_shape(cache_b, mode in ("moba", "mla")), F32),
                            pltpu.SemaphoreType.DMA((2, 2))],
        ),
        out_shape=jax.ShapeDtypeStruct((n_seq, out_rows, LANES), BF16),
        compiler_params=pltpu.CompilerParams(dimension_semantics=("arbitrary",), vmem_limit_bytes=VMEM_LIMIT_BYTES),
        name="decode_" + mode,
    )(page_table, q, knew, vnew, *extras, cache_a, cache_b)


def _merge_kernel(x_ref, oa_ref, ob_ref, oc_ref, wgt_ref, wa_ref, wb_ref, wc_ref, wo_ref, g_ref, b_ref, o_ref, *, alpha):
    x = x_ref[...]
    xb = x.astype(BF16)
    mix = None
    for br, (br_ref, w_ref) in enumerate(((oa_ref, wa_ref), (ob_ref, wb_ref), (oc_ref, wc_ref))):
        gate = jax.nn.sigmoid(_dot(xb, wgt_ref[:, br * D_MODEL:(br + 1) * D_MODEL]))
        y = gate * _dot(br_ref[...], w_ref[...])
        mix = y if mix is None else mix + y
    r = alpha * x + _dot(mix.astype(BF16), wo_ref[...])
    o_ref[...] = _layer_norm(r, g_ref[...], b_ref[...])


def _merge(x2d, oa, ob, oc, w_gt, w_a, w_b, w_c, w_o, g, b, *, tm, alpha):
    n = x2d.shape[0]
    row = lambda t: (t, 0)
    const = lambda t: (0, 0)
    full = lambda a: pl.BlockSpec(a.shape, const)
    return pl.pallas_call(
        functools.partial(_merge_kernel, alpha=alpha),
        grid=(n // tm,),
        in_specs=[pl.BlockSpec((tm, D_MODEL), row), pl.BlockSpec((tm, oa.shape[1]), row),
                  pl.BlockSpec((tm, ob.shape[1]), row), pl.BlockSpec((tm, oc.shape[1]), row),
                  full(w_gt), full(w_a), full(w_b), full(w_c), full(w_o), full(g), full(b)],
        out_specs=pl.BlockSpec((tm, D_MODEL), row),
        out_shape=jax.ShapeDtypeStruct((n, D_MODEL), F32),
        compiler_params=pltpu.CompilerParams(dimension_semantics=("parallel",), vmem_limit_bytes=VMEM_LIMIT_BYTES),
        name="merge",
    )(x2d, oa, ob, oc, w_gt, w_a, w_b, w_c, w_o, g, b)


def _router_gate(x, router):
    x_hi = x.astype(BF16)
    x_lo = (x - x_hi.astype(F32)).astype(BF16)
    r_hi = router.astype(BF16)
    r_lo = (router - r_hi.astype(F32)).astype(BF16)
    logits = _dot(x_hi, r_hi) + (_dot(x_hi, r_lo) + _dot(x_lo, r_hi))
    lane = lax.broadcasted_iota(jnp.int32, logits.shape, 1).astype(F32)
    lg = jnp.where(lane < N_EXPERTS, logits, NEG)
    m1 = jnp.max(lg, axis=-1, keepdims=True)
    i1 = jnp.min(jnp.where(lg == m1, lane, float(LANES)), axis=-1, keepdims=True)
    lg2 = jnp.where(lane == i1, NEG, lg)
    m2 = jnp.max(lg2, axis=-1, keepdims=True)
    i2 = jnp.min(jnp.where(lg2 == m2, lane, float(LANES)), axis=-1, keepdims=True)
    e2 = jnp.exp(m2 - m1)
    den = 1.0 + e2
    return jnp.where(lane == i1, 1.0 / den, 0.0) + jnp.where(lane == i2, e2 / den, 0.0)


def _ffn_kernel(*refs, alpha, moe):
    if moe:
        x_ref, router_ref, wg_ref, wu_ref, wd_ref, g_ref, b_ref, o_ref, acc_scr, xb_scr, gate_scr = refs
    else:
        x_ref, wg_ref, wu_ref, wd_ref, g_ref, b_ref, o_ref, acc_scr, xb_scr = refs
    e = pl.program_id(1)
    c = pl.program_id(2)
    first = jnp.logical_and(e == 0, c == 0)
    last = jnp.logical_and(e == pl.num_programs(1) - 1, c == pl.num_programs(2) - 1)

    @pl.when(first)
    def _init():
        x = x_ref[...]
        acc_scr[...] = jnp.zeros(acc_scr.shape, F32)
        xb_scr[...] = x.astype(BF16)
        if moe:
            gate_scr[...] = _router_gate(x, router_ref[...])

    xb = xb_scr[...]
    h = jax.nn.silu(_dot(xb, wg_ref[0])) * _dot(xb, wu_ref[0])
    if moe:
        gate = gate_scr[...]
        lane = lax.broadcasted_iota(jnp.int32, gate.shape, 1)
        h = h * jnp.sum(jnp.where(lane == e, gate, 0.0), axis=-1, keepdims=True)
    acc_scr[...] += _dot(h.astype(BF16), wd_ref[0])

    @pl.when(last)
    def _fin():
        o_ref[...] = _layer_norm(alpha * x_ref[...] + acc_scr[...], g_ref[...], b_ref[...])


def _ffn(x2d, router, wg, wu, wd, g, b, *, tm, fc, alpha):
    n = x2d.shape[0]
    n_e, _, f = wg.shape
    moe = router is not None
    row = lambda t, e, c: (t, 0)
    const = lambda t, e, c: (0, 0)
    in_specs = [pl.BlockSpec((tm, D_MODEL), row)]
    args = [x2d]
    if moe:
        in_specs.append(pl.BlockSpec(router.shape, const))
        args.append(router)
    in_specs += [pl.BlockSpec((1, D_MODEL, fc), lambda t, e, c: (e, 0, c)),
                 pl.BlockSpec((1, D_MODEL, fc), lambda t, e, c: (e, 0, c)),
                 pl.BlockSpec((1, fc, D_MODEL), lambda t, e, c: (e, c, 0)),
                 pl.BlockSpec(g.shape, const), pl.BlockSpec(b.shape, const)]
    args += [wg, wu, wd, g, b]
    scratch = [pltpu.VMEM((tm, D_MODEL), F32), pltpu.VMEM((tm, D_MODEL), BF16)]
    if moe:
        scratch.append(pltpu.VMEM((tm, LANES), F32))
    return pl.pallas_call(
        functools.partial(_ffn_kernel, alpha=alpha, moe=moe),
        grid=(n // tm, n_e, f // fc),
        in_specs=in_specs,
        out_specs=pl.BlockSpec((tm, D_MODEL), row),
        out_shape=jax.ShapeDtypeStruct((n, D_MODEL), F32),
        scratch_shapes=scratch,
        compiler_params=pltpu.CompilerParams(dimension_semantics=("parallel", "arbitrary", "arbitrary"),
                                             vmem_limit_bytes=VMEM_LIMIT_BYTES),
        name="moe_ffn" if moe else "ffn",
    )(*args)


def _rope_tables(pos):
    posf = pos.astype(F32)[:, None]

    def angles(dim):
        inv = ROPE_THETA ** (-jnp.arange(0, dim, 2, dtype=F32) / dim)
        ang = posf * inv[None, :]
        return jnp.cos(ang), jnp.sin(ang)

    lane = jnp.arange(LANES)
    cos_a, sin_a = angles(MOBA_ROPE_DIM)
    ha = MOBA_ROPE_DIM // 2
    ia = (lane % MOBA_HEAD_DIM)[None, :]
    ca = jnp.where(ia < 2 * ha, cos_a[:, lane % ha], 1.0)
    s1a = jnp.where(ia < ha, -sin_a[:, lane % ha], 0.0)
    s2a = jnp.where((ia >= ha) & (ia < 2 * ha), sin_a[:, lane % ha], 0.0)
    cos_b, sin_b = angles(MLA_ROPE_DIM)
    hb = MLA_ROPE_DIM // 2
    ib = (lane % MLA_ROPE_DIM)[None, :]
    cb = cos_b[:, lane % hb]
    s1b = jnp.where(ib < hb, -sin_b[:, lane % hb], 0.0)
    s2b = jnp.where(ib >= hb, sin_b[:, lane % hb], 0.0)
    return jnp.stack([ca, s1a, s2a, cb, s1b, s2b]).astype(F32)


def _pack_layer_weights(w_in_l, w_uq_l, w_uk_l, w_uv_l, w_br_moba_l, w_br_mla_l):
    pts, acc = [], 0
    for w in IN_WIDTHS[:-1]:
        acc += w
        pts.append(acc)
    mq, mk, mv, cq, ckv, kr, dq, dk, dv, gt = jnp.split(w_in_l, pts, axis=1)
    order = jnp.array(MOBA_HEAD_ORDER)
    mq = mq.reshape(D_MODEL, MOBA_HEADS, MOBA_HEAD_DIM)[:, order, :].reshape(D_MODEL, -1) * (MOBA_SCALE * LOG2E)
    kr = jnp.pad(kr, ((0, 0), (0, LANES - MLA_ROPE_DIM)))
    w_proj = jnp.concatenate([mq, mk, mv, cq, ckv, kr, dq * (DIFF_SCALE * LOG2E), dk, dv], axis=1).astype(BF16)
    w_gt = gt.astype(BF16)

    qd = MLA_NOPE_DIM + MLA_ROPE_DIM
    uq = w_uq_l.reshape(MLA_Q_RANK, MLA_HEADS, qd)
    uq_nope = jnp.transpose(uq[:, :, :MLA_NOPE_DIM], (1, 0, 2))
    uk_t = jnp.transpose(w_uk_l, (1, 2, 0))
    w_abs = _fold(uq_nope, uk_t)
    uq_rope = jnp.transpose(uq[:, :, MLA_NOPE_DIM:], (1, 0, 2))
    uq_rope = jnp.pad(uq_rope, ((0, 0), (0, 0), (0, LANES - MLA_ROPE_DIM)))
    w_q2 = jnp.concatenate([w_abs, uq_rope], axis=2) * (MLA_SCALE * LOG2E)
    w_q2 = jnp.transpose(w_q2, (1, 0, 2)).reshape(MLA_Q_RANK, MLA_HEADS * 2 * LANES).astype(BF16)

    uv = jnp.transpose(w_uv_l, (1, 0, 2))
    br = w_br_mla_l.reshape(MLA_HEADS, MLA_V_DIM, D_MODEL)
    w_ob = _fold(uv, br).reshape(MLA_HEADS * MLA_KV_RANK, D_MODEL).astype(BF16)

    w_a = w_br_moba_l.reshape(MOBA_HEADS, MOBA_HEAD_DIM, D_MODEL)[order].reshape(-1, D_MODEL).astype(BF16)
    return w_proj, w_gt, w_q2, w_ob, w_a


def _pick_tile(n, pref):
    t = min(n, pref)
    while n % t:
        t //= 2
    return t


def _pick_ff_chunk(f, cap=1536):
    best = LANES
    for c in range(LANES, min(f, cap) + 1, LANES):
        if f % c == 0:
            best = c
    return best


def kernel(x_prompt, x_sample, cache_moba_k, cache_moba_v, cache_mla_ckv, cache_mla_krope, cache_diff_k, cache_diff_v, page_table, w_in, mla_q_norm, mla_w_uq, mla_kv_norm, mla_w_uk, mla_w_uv, diff_lambda_q1, diff_lambda_k1, diff_lambda_q2, diff_lambda_k2, diff_subln, w_br_moba, w_br_mla, w_br_diff, w_o, ln1_g, ln1_b, ln2_g, ln2_b, ffn_w_gate, ffn_w_up, ffn_w_down, moe_router, moe_w_gate, moe_w_up, moe_w_down):
    bsz, seq, _ = x_prompt.shape
    n_seq, dec_t, _ = x_sample.shape
    assert dec_t == 1 and seq % MOBA_BLOCK == 0
    depth = w_in.shape[0]
    n_pool = cache_moba_k.shape[1]
    n_pages = page_table.shape[1]
    past = n_pages * PAGE_SIZE
    alpha = (2 * depth) ** 0.25
    n_p = bsz * seq

    kt_moba = jnp.transpose(cache_moba_k, (0, 1, 3, 4, 2)).reshape(depth, n_pool, LANES, PAGE_SIZE)
    vt_moba = jnp.transpose(cache_moba_v, (0, 1, 3, 4, 2)).reshape(depth, n_pool, LANES, PAGE_SIZE)
    krt_mla = jnp.transpose(cache_mla_krope, (0, 1, 3, 2))
    kt_diff = jnp.transpose(cache_diff_k, (0, 1, 3, 4, 5, 2)).reshape(depth, n_pool, LANES, PAGE_SIZE)
    v_diff = cache_diff_v.reshape(depth, n_pool, PAGE_SIZE, LANES)

    tabs_p = _rope_tables(jnp.arange(seq, dtype=jnp.int32))
    tabs_s = jnp.broadcast_to(_rope_tables(past + jnp.arange(1, dtype=jnp.int32)), (6, n_seq, LANES))

    tm_p = _pick_tile(seq, 512)
    tq_big = _pick_tile(seq, 512)
    tm_ffn = _pick_tile(n_p, 512)

    xp = x_prompt.reshape(n_p, D_MODEL)
    xs = x_sample.reshape(n_seq, D_MODEL)
    leaves = [[] for _ in range(12)]

    for l in range(depth):
        lam_init = 0.8 - 0.6 * math.exp(-0.3 * l)
        w_proj, w_gt, w_q2, w_ob, w_a = _pack_layer_weights(
            w_in[l], mla_w_uq[l], mla_w_uk[l], mla_w_uv[l], w_br_moba[l], w_br_mla[l])
        w_c = w_br_diff[l].astype(BF16)
        w_out = w_o[l].astype(BF16)
        qn = mla_q_norm[l][None, :]
        kvn = mla_kv_norm[l][None, :]
        lam_rows = jnp.concatenate([jnp.stack([diff_lambda_q1[l], diff_lambda_k1[l], diff_lambda_q2[l], diff_lambda_k2[l]]),
                                    jnp.zeros((4, DIFF_HEAD_DIM), F32)]).astype(F32)
        subln = diff_subln[l][None, :]

        (mq, mk, mv, mkb, mvt, qcat, ckv, kr, kcat, ckvt, dq, dk, dv, dkb, dvt, kmean) = _in_proj(
            xp, w_proj, w_q2, qn, kvn, tabs_p, tm=tm_p, with_kmean=True)
        nb = seq // MOBA_BLOCK
        kmean_b = jnp.pad(kmean.reshape(bsz, nb, LANES), ((0, 0), (0, LANES - nb), (0, 0))).astype(BF16)
        r3 = lambda a: a.reshape(bsz, seq, a.shape[1])
        oa_p = _flash(r3(mq), r3(mkb), mvt, (kmean_b,), mode="moba", tq=tq_big)
        ob_p = _flash(r3(qcat), r3(kcat), ckvt, (), mode="mla", tq=tq_big)
        oc_p = _flash(r3(dq), r3(dkb), dvt, (lam_rows, subln), mode="diff", tq=tq_big, lam_init=lam_init)

        (mq_s, mk_s, mv_s, _, _, qcat_s, ckv_s, kr_s, _, _, dq_s, dk_s, dv_s, _, _) = _in_proj(
            xs, w_proj, w_q2, qn, kvn, tabs_s, tm=n_seq, with_kmean=False)
        q3 = lambda a: a.reshape(n_seq, N_STREAMS, a.shape[1] // N_STREAMS)
        r1 = lambda a: a[:, None, :]
        kcat_s = jnp.concatenate([ckv_s, kr_s, jnp.zeros((n_seq, LANES - MLA_ROPE_DIM), F32)], axis=1)
        oa_s = _decode(page_table, q3(mq_s), r1(mk_s), r1(mv_s), (), kt_moba, vt_moba, mode="moba", layer=l)
        ob_s = _decode(page_table, q3(qcat_s), r1(kcat_s), r1(ckv_s), (), cache_mla_ckv, krt_mla, mode="mla", layer=l)
        oc_s = _decode(page_table, q3(dq_s), r1(dk_s), r1(dv_s), (lam_rows, subln), kt_diff, v_diff,
                       mode="diff", layer=l, lam_init=lam_init)

        g1, b1 = ln1_g[l][None, :], ln1_b[l][None, :]
        g2, b2 = ln2_g[l][None, :], ln2_b[l][None, :]
        xp = _merge(xp, oa_p.reshape(n_p, -1), ob_p.reshape(n_p, -1), oc_p.reshape(n_p, -1),
                    w_gt, w_a, w_ob, w_c, w_out, g1, b1, tm=tm_p, alpha=alpha)
        xs = _merge(xs, oa_s.reshape(n_seq, -1), ob_s.reshape(n_seq, -1), oc_s.reshape(n_seq, -1),
                    w_gt, w_a, w_ob, w_c, w_out, g1, b1, tm=n_seq, alpha=alpha)
        j = l // 2
        if l % 2 == 0:
            router = None
            wg, wu, wd = (ffn_w_gate[j][None].astype(BF16), ffn_w_up[j][None].astype(BF16),
                          ffn_w_down[j][None].astype(BF16))
        else:
            router = jnp.pad(moe_router[j], ((0, 0), (0, LANES - N_EXPERTS)))
            wg, wu, wd = moe_w_gate[j].astype(BF16), moe_w_up[j].astype(BF16), moe_w_down[j].astype(BF16)
        fc = _pick_ff_chunk(wg.shape[2])
        xp = _ffn(xp, router, wg, wu, wd, g2, b2, tm=tm_ffn, fc=fc, alpha=alpha)
        xs = _ffn(xs, router, wg, wu, wd, g2, b2, tm=n_seq, fc=fc, alpha=alpha)

        p5 = lambda a, *tail: a.reshape(bsz, seq, *tail)
        s5 = lambda a, *tail: a.reshape(n_seq, 1, *tail)
        new = [p5(mk, MOBA_KV_HEADS, MOBA_HEAD_DIM), s5(mk_s, MOBA_KV_HEADS, MOBA_HEAD_DIM),
               p5(mv, MOBA_KV_HEADS, MOBA_HEAD_DIM), s5(mv_s, MOBA_KV_HEADS, MOBA_HEAD_DIM),
               p5(ckv, MLA_KV_RANK), s5(ckv_s, MLA_KV_RANK),
               p5(kr, MLA_ROPE_DIM), s5(kr_s, MLA_ROPE_DIM),
               p5(dk, DIFF_KV_HEADS, 2, DIFF_HEAD_DIM), s5(dk_s, DIFF_KV_HEADS, 2, DIFF_HEAD_DIM),
               p5(dv, DIFF_KV_HEADS, 2 * DIFF_HEAD_DIM), s5(dv_s, DIFF_KV_HEADS, 2 * DIFF_HEAD_DIM)]
        for acc_list, a in zip(leaves, new):
            acc_list.append(a)

    return (xp.reshape(bsz, seq, D_MODEL), xs.reshape(n_seq, 1, D_MODEL)) + tuple(jnp.stack(a) for a in leaves)
```

```python
import functools
import math

import jax
import jax.numpy as jnp
from jax import lax
from jax.experimental import pallas as pl
from jax.experimental.pallas import tpu as pltpu

F32 = jnp.float32
BF16 = jnp.bfloat16

D_MODEL = 1024
PAGE_SIZE = 128
MOBA_HEADS = 8
MOBA_KV_HEADS = 2
MOBA_HEAD_DIM = 64
MOBA_BLOCK = 256
MOBA_TOPK = 3
MOBA_ROPE_DIM = MOBA_HEAD_DIM // 4
MOBA_SCALE = MOBA_HEAD_DIM ** -0.5
MLA_HEADS = 8
MLA_Q_RANK = 256
MLA_KV_RANK = 128
MLA_NOPE_DIM = 64
MLA_ROPE_DIM = 32
MLA_V_DIM = 64
MLA_SCALE = (MLA_NOPE_DIM + MLA_ROPE_DIM) ** -0.5
DIFF_HEADS = 4
DIFF_KV_HEADS = 1
DIFF_HEAD_DIM = 64
DIFF_ROPE_DIM = DIFF_HEAD_DIM // 4
DIFF_SCALE = DIFF_HEAD_DIM ** -0.5
N_BRANCH = 3
ROPE_THETA = 500000.0
LN_EPS = 1e-5
RMS_EPS = 1e-6
DIFF_SUBLN_EPS = 1e-5
N_EXPERTS = 8
IN_WIDTHS = (MOBA_HEADS * MOBA_HEAD_DIM, MOBA_KV_HEADS * MOBA_HEAD_DIM, MOBA_KV_HEADS * MOBA_HEAD_DIM,
             MLA_Q_RANK, MLA_KV_RANK, MLA_ROPE_DIM,
             DIFF_HEADS * 2 * DIFF_HEAD_DIM, DIFF_KV_HEADS * 2 * DIFF_HEAD_DIM, DIFF_KV_HEADS * 2 * DIFF_HEAD_DIM,
             N_BRANCH * D_MODEL)

LANES = 128
MXU_TILE = 256
VMEM_LIMIT_BYTES = 56 * 1024 * 1024
N_STREAMS = 8
NEG = -1e30
LOG2E = math.log2(math.e)
ONES_ROWS = 16
MOBA_HEAD_ORDER = (0, 4, 1, 5, 2, 6, 3, 7)

_C_MQ, _C_MK, _C_MV, _C_CQ, _C_CKV, _C_KR, _C_DQ, _C_DK, _C_DV, _C_END = (
    0, 512, 640, 768, 1024, 1152, 1280, 1792, 1920, 2048)


def _nt_dot(a, b):
    return lax.dot_general(a, b, (((1,), (1,)), ((), ())), preferred_element_type=F32)


def _dot(a, b):
    return jnp.dot(a, b, preferred_element_type=F32)


def _layer_norm(r, g, b):
    mu = jnp.mean(r, axis=-1, keepdims=True)
    d = r - mu
    var = jnp.mean(d * d, axis=-1, keepdims=True)
    return d * lax.rsqrt(var + LN_EPS) * g + b


def _rope128(h, cos, s1, s2, half):
    return h * cos + pltpu.roll(h, LANES - half, 1) * s1 + pltpu.roll(h, half, 1) * s2


def _fold_kernel(a_ref, b_ref, o_ref):
    o_ref[0] = jnp.dot(a_ref[0], b_ref[0], preferred_element_type=F32, precision=lax.Precision.HIGHEST)


def _fold(a, b):
    nh, m, k = a.shape
    n = b.shape[2]
    return pl.pallas_call(
        _fold_kernel,
        grid=(nh,),
        in_specs=[pl.BlockSpec((1, m, k), lambda h: (h, 0, 0)), pl.BlockSpec((1, k, n), lambda h: (h, 0, 0))],
        out_specs=pl.BlockSpec((1, m, n), lambda h: (h, 0, 0)),
        out_shape=jax.ShapeDtypeStruct((nh, m, n), F32),
        name="fold_weights",
    )(a, b)


def _in_proj_kernel(x_ref, w_ref, wq2_ref, qn_ref, kvn_ref, tab_ref,
                    mq_ref, mk_ref, mv_ref, mkb_ref, mvt_ref, qcat_ref, ckv_ref, kr_ref, kcat_ref, ckvt_ref,
                    dq_ref, dk_ref, dv_ref, dkb_ref, dvt_ref, *kmean_refs):
    tm = x_ref.shape[0]
    xb = x_ref[...].astype(BF16)
    cos_a, s1_a, s2_a = tab_ref[0], tab_ref[1], tab_ref[2]
    cos_b, s1_b, s2_b = tab_ref[3], tab_ref[4], tab_ref[5]
    half_a = MOBA_ROPE_DIM // 2
    half_b = MLA_ROPE_DIM // 2
    lo = lax.broadcasted_iota(jnp.int32, (tm, LANES), 1) < (LANES // 2)

    def proj(c0, c1):
        return _dot(xb, w_ref[:, c0:c1])

    def rope_a(h):
        return _rope128(h, cos_a, s1_a, s2_a, half_a)

    def expand_halves(h, n_slabs):
        pieces = []
        for v in range(n_slabs):
            hv = rope_a(h[:, v * LANES:(v + 1) * LANES])
            pieces.append(jnp.where(lo, hv, 0.0))
            pieces.append(jnp.where(lo, 0.0, hv))
        return jnp.concatenate(pieces, axis=1).astype(BF16)

    mq_ref[...] = expand_halves(proj(_C_MQ, _C_MK), 4)
    mk = rope_a(proj(_C_MK, _C_MV))
    mk_ref[...] = mk
    mkb_ref[...] = mk.astype(BF16)
    mv = proj(_C_MV, _C_CQ)
    mv_ref[...] = mv
    mvt_ref[...] = mv.T.astype(BF16)
    if kmean_refs:
        for blk in range(tm // MOBA_BLOCK):
            kmean_refs[0][blk] = jnp.mean(mk[blk * MOBA_BLOCK:(blk + 1) * MOBA_BLOCK], axis=0, keepdims=True)

    cq = proj(_C_CQ, _C_CKV)
    cqn = cq * lax.rsqrt(jnp.mean(cq * cq, axis=-1, keepdims=True) + RMS_EPS) * qn_ref[...]
    q2 = _dot(cqn.astype(BF16), wq2_ref[...])
    pieces = []
    for h in range(MLA_HEADS):
        pieces.append(q2[:, (2 * h) * LANES:(2 * h + 1) * LANES])
        pieces.append(_rope128(q2[:, (2 * h + 1) * LANES:(2 * h + 2) * LANES], cos_b, s1_b, s2_b, half_b))
    qcat_ref[...] = jnp.concatenate(pieces, axis=1).astype(BF16)
    ckv = proj(_C_CKV, _C_KR)
    ckv = ckv * lax.rsqrt(jnp.mean(ckv * ckv, axis=-1, keepdims=True) + RMS_EPS) * kvn_ref[...]
    ckv_ref[...] = ckv
    kr = _rope128(proj(_C_KR, _C_DQ), cos_b, s1_b, s2_b, half_b)
    kr_ref[...] = kr[:, :MLA_ROPE_DIM]
    kcat_ref[...] = jnp.concatenate([ckv, kr], axis=1).astype(BF16)
    ckvt_ref[...] = ckv.T.astype(BF16)

    dq_ref[...] = expand_halves(proj(_C_DQ, _C_DK), 4)
    dk = rope_a(proj(_C_DK, _C_DV))
    dk_ref[...] = dk
    dkb_ref[...] = dk.astype(BF16)
    dv = proj(_C_DV, _C_END)
    dv_ref[...] = dv
    dvt_ref[...] = dv.T.astype(BF16)


def _in_proj(x2d, w_proj, w_q2, q_norm, kv_norm, tabs, *, tm, with_kmean):
    n = x2d.shape[0]
    n_pos_tiles = tabs.shape[1] // tm
    row = lambda t: (t, 0)
    const = lambda t: (0, 0)

    def rows(width, dtype):
        return jax.ShapeDtypeStruct((n, width), dtype), pl.BlockSpec((tm, width), row)

    transposed = (jax.ShapeDtypeStruct((LANES, n), BF16), pl.BlockSpec((LANES, tm), lambda t: (0, t)))
    outs = [rows(1024, BF16), rows(128, F32), rows(128, F32), rows(128, BF16), transposed,
            rows(2048, BF16), rows(128, F32), rows(MLA_ROPE_DIM, F32), rows(256, BF16), transposed,
            rows(1024, BF16), rows(128, F32), rows(128, F32), rows(128, BF16), transposed]
    if with_kmean:
        nblk = tm // MOBA_BLOCK
        outs.append((jax.ShapeDtypeStruct((n // MOBA_BLOCK, 1, LANES), F32),
                     pl.BlockSpec((nblk, 1, LANES), lambda t: (t, 0, 0))))
    return pl.pallas_call(
        _in_proj_kernel,
        grid=(n // tm,),
        in_specs=[pl.BlockSpec((tm, D_MODEL), row),
                  pl.BlockSpec(w_proj.shape, const),
                  pl.BlockSpec(w_q2.shape, const),
                  pl.BlockSpec((1, MLA_Q_RANK), const),
                  pl.BlockSpec((1, MLA_KV_RANK), const),
                  pl.BlockSpec((6, tm, LANES), lambda t: (0, t % n_pos_tiles, 0))],
        out_specs=[o[1] for o in outs],
        out_shape=[o[0] for o in outs],
        compiler_params=pltpu.CompilerParams(dimension_semantics=("parallel",), vmem_limit_bytes=VMEM_LIMIT_BYTES),
        name="in_proj",
    )(x2d, w_proj, w_q2, q_norm, kv_norm, tabs)


def _top3_bias(gate, n_valid, own):
    lane = lax.broadcasted_iota(jnp.int32, gate.shape, 1).astype(F32)
    g = jnp.where(lane < n_valid, gate, NEG)
    bias = jnp.where(lane == own, 0.0, NEG)
    for _ in range(MOBA_TOPK):
        mx = jnp.max(g, axis=-1, keepdims=True)
        idx = jnp.min(jnp.where(g == mx, lane, float(LANES)), axis=-1, keepdims=True)
        hit = lane == idx
        bias = jnp.where(jnp.logical_and(hit, mx > 0.5 * NEG), 0.0, bias)
        g = jnp.where(hit, NEG, g)
    return bias


def _diff_lambda(lam_ref, lam_init):
    a = jnp.sum(lam_ref[0:1, :] * lam_ref[1:2, :], axis=-1, keepdims=True)
    b = jnp.sum(lam_ref[2:3, :] * lam_ref[3:4, :], axis=-1, keepdims=True)
    return jnp.exp(a) - jnp.exp(b) + lam_init


def _diff_combine(o0, o1, lam, subln, lam_init):
    o = o0 - lam * o1
    o = o * lax.rsqrt(jnp.mean(o * o, axis=-1, keepdims=True) + DIFF_SUBLN_EPS) * subln
    return o * (1.0 - lam_init)


def _flash_kernel(*refs, mode, dq, lam_init):
    i_tab, j_tab = refs[0], refs[1]
    if mode == "moba":
        q_ref, k_ref, vt_ref, kmean_ref, o_ref, m_scr, acc_scr, qaug_scr = refs[2:]
    elif mode == "diff":
        q_ref, k_ref, vt_ref, lam_ref, subln_ref, o_ref, m_scr, acc_scr = refs[2:]
    else:
        q_ref, k_ref, vt_ref, o_ref, m_scr, acc_scr = refs[2:]
    tq = q_ref.shape[1]
    tk = k_ref.shape[1]
    i = i_tab[pl.program_id(1)]
    j = j_tab[pl.program_id(1)]
    blocks_per_tile = tq // MOBA_BLOCK
    block_shift = MOBA_BLOCK.bit_length() - 1

    @pl.when(j == 0)
    def _init():
        m_scr[...] = jnp.full(m_scr.shape, NEG, F32)
        acc_scr[...] = jnp.zeros(acc_scr.shape, F32)
        if mode == "moba":
            km = kmean_ref[0]
            row_block = lax.shift_right_logical(lax.broadcasted_iota(jnp.int32, (tq, LANES), 0), block_shift)
            own = (i * blocks_per_tile + row_block).astype(F32)
            for s in range(N_STREAMS):
                qs = q_ref[0, :, s * dq:(s + 1) * dq]
                bias = _top3_bias(_nt_dot(qs, km), own, own)
                qaug_scr[s, :, 0:dq] = qs
                qaug_scr[s, :, dq:2 * dq] = bias.astype(BF16)

    def step(diagonal):
        k = k_ref[0]
        vt = vt_ref[...]
        if mode == "moba":
            key_block = j * blocks_per_tile + lax.shift_right_logical(
                lax.broadcasted_iota(jnp.int32, (tk, LANES), 0), block_shift)
            onehot = jnp.where(lax.broadcasted_iota(jnp.int32, (tk, LANES), 1) == key_block, 1.0, 0.0).astype(BF16)
            k = jnp.concatenate([k, onehot], axis=1)
        if diagonal:
            causal = (lax.broadcasted_iota(jnp.int32, (tk, tq), 0)
                      <= lax.broadcasted_iota(jnp.int32, (tk, tq), 1))
        vt = jnp.concatenate([vt, jnp.ones((ONES_ROWS, tk), BF16)], axis=0)

        def scores(s):
            if mode == "moba":
                qs = qaug_scr[s]
            else:
                qs = q_ref[0, :, s * dq:(s + 1) * dq]
            st = _nt_dot(k, qs)
            if diagonal:
                st = jnp.where(causal, st, NEG)
            m_prev = m_scr[s]
            m_new = jnp.maximum(m_prev, jnp.max(st, axis=0, keepdims=True))
            return st, m_prev, m_new

        def probs(s, st, m_prev, m_new):
            m_scr[s] = m_new
            return jnp.exp2(m_prev - m_new), jnp.exp2(st - m_new).astype(BF16)

        def accumulate(s, alpha, p):
            acc_scr[s] = alpha * acc_scr[s] + _dot(vt, p)

        sc = {0: scores(0), 1: scores(1)}
        pr = {0: probs(0, *sc.pop(0))}
        for s in range(N_STREAMS):
            if s + 2 < N_STREAMS:
                sc[s + 2] = scores(s + 2)
            if s + 1 < N_STREAMS:
                pr[s + 1] = probs(s + 1, *sc.pop(s + 1))
            accumulate(s, *pr.pop(s))

    @pl.when(j < i)
    def _past():
        step(False)

    @pl.when(j == i)
    def _diag():
        step(True)
        outs_t = [acc_scr[s, 0:LANES, :] / acc_scr[s, LANES:LANES + 1, :] for s in range(N_STREAMS)]
        if mode == "mla":
            o_ref[0] = jnp.concatenate([o.T for o in outs_t], axis=1).astype(o_ref.dtype)
        elif mode == "moba":
            half = LANES // 2
            o_ref[0] = jnp.concatenate(
                [jnp.concatenate([outs_t[2 * v][:half], outs_t[2 * v + 1][half:]], axis=0).T
                 for v in range(N_STREAMS // 2)], axis=1).astype(o_ref.dtype)
        else:
            lam = _diff_lambda(lam_ref, lam_init)
            pieces = []
            for h in range(DIFF_HEADS):
                o = outs_t[2 * h] - lam * outs_t[2 * h + 1]
                o = o * lax.rsqrt(jnp.mean(o * o, axis=0, keepdims=True) + DIFF_SUBLN_EPS)
                pieces.append(o.T * subln_ref[...] * (1.0 - lam_init))
            o_ref[0] = jnp.concatenate(pieces, axis=1).astype(o_ref.dtype)


def _flash(q, k, vt, extras, *, mode, tq, lam_init=0.0):
    b, s, qw = q.shape
    dq = qw // N_STREAMS
    dk = k.shape[2]
    n_t = s // tq
    out_w = N_STREAMS * LANES if mode == "mla" else (N_STREAMS // 2) * LANES
    pairs = [(i, j) for i in range(n_t) for j in range(i + 1)]
    i_tab = jnp.array([p[0] for p in pairs], jnp.int32)
    j_tab = jnp.array([p[1] for p in pairs], jnp.int32)
    const2 = lambda bb, p, it, jt: (0, 0)
    in_specs = [pl.BlockSpec((1, tq, qw), lambda bb, p, it, jt: (bb, it[p], 0)),
                pl.BlockSpec((1, tq, dk), lambda bb, p, it, jt: (bb, jt[p], 0)),
                pl.BlockSpec((LANES, tq), lambda bb, p, it, jt: (0, bb * n_t + jt[p]))]
    scratch = [pltpu.VMEM((N_STREAMS, 1, tq), F32), pltpu.VMEM((N_STREAMS, LANES + ONES_ROWS, tq), F32)]
    if mode == "moba":
        assert tq % MOBA_BLOCK == 0 and s // MOBA_BLOCK <= LANES and MOBA_BLOCK & (MOBA_BLOCK - 1) == 0
        in_specs.append(pl.BlockSpec((1, LANES, LANES), lambda bb, p, it, jt: (bb, 0, 0)))
        scratch.append(pltpu.VMEM((N_STREAMS, tq, 2 * dq), BF16))
    elif mode == "diff":
        in_specs.append(pl.BlockSpec((8, DIFF_HEAD_DIM), const2))
        in_specs.append(pl.BlockSpec((1, LANES), const2))
    return pl.pallas_call(
        functools.partial(_flash_kernel, mode=mode, dq=dq, lam_init=lam_init),
        grid_spec=pltpu.PrefetchScalarGridSpec(
            num_scalar_prefetch=2,
            grid=(b, len(pairs)),
            in_specs=in_specs,
            out_specs=pl.BlockSpec((1, tq, out_w), lambda bb, p, it, jt: (bb, it[p], 0)),
            scratch_shapes=scratch,
        ),
        out_shape=jax.ShapeDtypeStruct((b, s, out_w), BF16),
        compiler_params=pltpu.CompilerParams(dimension_semantics=("parallel", "arbitrary"),
                                             vmem_limit_bytes=VMEM_LIMIT_BYTES),
        name="flash_" + mode,
    )(i_tab, j_tab, q, k, vt, *extras)


def _decode_kernel(*refs, mode, layer, n_pages, chunk, lam_init):
    pt_ref = refs[0]
    if mode == "diff":
        q_ref, knew_ref, vnew_ref, lam_ref, subln_ref, ca_hbm, cb_hbm, o_ref, buf_a, buf_b, sems = refs[1:]
    else:
        q_ref, knew_ref, vnew_ref, ca_hbm, cb_hbm, o_ref, buf_a, buf_b, sems = refs[1:]
    a_transposed = mode in ("moba", "diff")
    b_transposed = mode in ("moba", "mla")
    n_seq = pl.num_programs(0)
    sq = pl.program_id(0)
    slot = sq % 2
    seq_len = n_pages * PAGE_SIZE

    def page_copy(src, dst, transposed, seq, p, sl, which):
        pg = pt_ref[seq, p]
        off = pl.multiple_of(p * PAGE_SIZE, PAGE_SIZE)
        if transposed:
            dst_view = dst.at[sl, :, pl.ds(off, PAGE_SIZE)]
        else:
            dst_view = dst.at[sl, pl.ds(off, PAGE_SIZE), :]
        return pltpu.make_async_copy(src.at[layer, pg], dst_view, sems.at[sl, which])

    def start_seq(seq, sl):
        def body(p2, c):
            for parity in range(2):
                p = 2 * p2 + parity
                page_copy(ca_hbm, buf_a, a_transposed, seq, p, sl, 0).start(priority=parity)
                page_copy(cb_hbm, buf_b, b_transposed, seq, p, sl, 1).start(priority=1 - parity)
            return c
        lax.fori_loop(0, n_pages // 2, body, 0)

    def wait_seq(seq, sl):
        def body(p, c):
            page_copy(ca_hbm, buf_a, a_transposed, seq, p, sl, 0).wait()
            page_copy(cb_hbm, buf_b, b_transposed, seq, p, sl, 1).wait()
            return c
        lax.fori_loop(0, n_pages, body, 0)

    @pl.when(sq == 0)
    def _prime():
        start_seq(sq, slot)

    @pl.when(sq + 1 < n_seq)
    def _prefetch():
        start_seq(sq + 1, 1 - slot)

    wait_seq(sq, slot)

    n_chunks = seq_len // chunk
    q = q_ref[0]
    qf = q.astype(F32)
    knew = knew_ref[0]
    vnew = vnew_ref[0]
    s_new = jnp.sum(qf * knew, axis=-1, keepdims=True)

    def a_chunk(c):
        if a_transposed:
            return buf_a[slot, :, c * chunk:(c + 1) * chunk].astype(BF16)
        return buf_a[slot, c * chunk:(c + 1) * chunk, :].astype(BF16)

    def b_chunk(c):
        if b_transposed:
            return buf_b[slot, :, c * chunk:(c + 1) * chunk].astype(BF16)
        return buf_b[slot, c * chunk:(c + 1) * chunk, :].astype(BF16)

    if mode == "mla":
        q_lat = q[:, 0:MLA_KV_RANK]
        q_rot = q[:, MLA_KV_RANK:MLA_KV_RANK + MLA_ROPE_DIM]
        sc = jnp.concatenate([_nt_dot(q_lat, a_chunk(c)) + _dot(q_rot, b_chunk(c)) for c in range(n_chunks)], axis=1)
    else:
        sc = jnp.concatenate([_dot(q, a_chunk(c)) for c in range(n_chunks)], axis=1)

    if mode == "moba":
        n_blocks = seq_len // MOBA_BLOCK
        lane = lax.broadcasted_iota(jnp.int32, (N_STREAMS, LANES), 1)
        gate = jnp.zeros((N_STREAMS, LANES), F32)
        for nb in range(n_blocks):
            tot = jnp.sum(sc[:, nb * MOBA_BLOCK:(nb + 1) * MOBA_BLOCK], axis=-1, keepdims=True)
            gate = jnp.where(lane == nb, tot, gate)
        bias = _top3_bias(gate, float(n_blocks), float(LANES))
        sc = jnp.concatenate(
            [sc[:, nb * MOBA_BLOCK:(nb + 1) * MOBA_BLOCK] + bias[:, nb:nb + 1] for nb in range(n_blocks)], axis=1)

    m = jnp.maximum(jnp.max(sc, axis=-1, keepdims=True), s_new)
    p = jnp.exp2(sc - m)
    p_new = jnp.exp2(s_new - m)
    l = jnp.sum(p, axis=-1, keepdims=True) + p_new
    pb = p.astype(BF16)
    acc = p_new * vnew
    for c in range(n_chunks):
        pc = pb[:, c * chunk:(c + 1) * chunk]
        if mode == "moba":
            acc = acc + _nt_dot(pc, b_chunk(c))
        elif mode == "mla":
            acc = acc + _dot(pc, a_chunk(c))
        else:
            acc = acc + _dot(pc, b_chunk(c))
    o = acc / l

    if mode == "mla":
        o_ref[0] = o.astype(o_ref.dtype)
    elif mode == "moba":
        lo = lax.broadcasted_iota(jnp.int32, (1, LANES), 1) < (LANES // 2)
        o_ref[0] = jnp.concatenate(
            [jnp.where(lo, o[2 * v:2 * v + 1], o[2 * v + 1:2 * v + 2]) for v in range(N_STREAMS // 2)],
            axis=0).astype(o_ref.dtype)
    else:
        lam = _diff_lambda(lam_ref, lam_init)
        o_ref[0] = jnp.concatenate(
            [_diff_combine(o[2 * h:2 * h + 1], o[2 * h + 1:2 * h + 2], lam, subln_ref[...], lam_init)
             for h in range(DIFF_HEADS)], axis=0).astype(o_ref.dtype)


def _decode(page_table, q, knew, vnew, extras, cache_a, cache_b, *, mode, layer, lam_init=0.0):
    n_seq, n_pages = page_table.shape
    seq_len = n_pages * PAGE_SIZE
    assert seq_len % MOBA_BLOCK == 0 and seq_len // MOBA_BLOCK < LANES
    dq = q.shape[2]
    out_rows = N_STREAMS if mode == "mla" else N_STREAMS // 2
    chunk = min(seq_len, 2048)

    def buf_shape(cache, transposed):
        feat = cache.shape[2] if transposed else cache.shape[3]
        return (2, feat, seq_len) if transposed else (2, seq_len, feat)

    in_specs = [pl.BlockSpec((1, N_STREAMS, dq), lambda s, pt: (s, 0, 0)),
                pl.BlockSpec((1, 1, dq), lambda s, pt: (s, 0, 0)),
                pl.BlockSpec((1, 1, LANES), lambda s, pt: (s, 0, 0))]
    if mode == "diff":
        in_specs.append(pl.BlockSpec((8, DIFF_HEAD_DIM), lambda s, pt: (0, 0)))
        in_specs.append(pl.BlockSpec((1, LANES), lambda s, pt: (0, 0)))
    in_specs += [pl.BlockSpec(memory_space=pl.ANY), pl.BlockSpec(memory_space=pl.ANY)]
    return pl.pallas_call(
        functools.partial(_decode_kernel, mode=mode, layer=layer, n_pages=n_pages, chunk=chunk, lam_init=lam_init),
        grid_spec=pltpu.PrefetchScalarGridSpec(
            num_scalar_prefetch=1,
            grid=(n_seq,),
            in_specs=in_specs,
            out_specs=pl.BlockSpec((1, out_rows, LANES), lambda s, pt: (s, 0, 0)),
            scratch_shapes=[pltpu.VMEM(buf_shape(cache_a, mode in ("moba", "diff")), F32),
                            pltpu.VMEM(buf_shape(cache_b, mode in ("moba", "mla")), F32),
                            pltpu.SemaphoreType.DMA((2, 2))],
        ),
        out_shape=jax.ShapeDtypeStruct((n_seq, out_rows, LANES), BF16),
        compiler_params=pltpu.CompilerParams(dimension_semantics=("arbitrary",), vmem_limit_bytes=VMEM_LIMIT_BYTES),
        name="decode_" + mode,
    )(page_table, q, knew, vnew, *extras, cache_a, cache_b)


def _merge_kernel(x_ref, oa_ref, ob_ref, oc_ref, wgt_ref, wa_ref, wb_ref, wc_ref, wo_ref, g_ref, b_ref, o_ref, *, alpha):
    x = x_ref[...]
    xb = x.astype(BF16)
    mix = None
    for br, (br_ref, w_ref) in enumerate(((oa_ref, wa_ref), (ob_ref, wb_ref), (oc_ref, wc_ref))):
        gate = jax.nn.sigmoid(_dot(xb, wgt_ref[:, br * D_MODEL:(br + 1) * D_MODEL]))
        y = gate * _dot(br_ref[...], w_ref[...])
        mix = y if mix is None else mix + y
    r = alpha * x + _dot(mix.astype(BF16), wo_ref[...])
    o_ref[...] = _layer_norm(r, g_ref[...], b_ref[...])


def _merge(x2d, oa, ob, oc, w_gt, w_a, w_b, w_c, w_o, g, b, *, tm, alpha):
    n = x2d.shape[0]
    row = lambda t: (t, 0)
    const = lambda t: (0, 0)
    full = lambda a: pl.BlockSpec(a.shape, const)
    return pl.pallas_call(
        functools.partial(_merge_kernel, alpha=alpha),
        grid=(n // tm,),
        in_specs=[pl.BlockSpec((tm, D_MODEL), row), pl.BlockSpec((tm, oa.shape[1]), row),
                  pl.BlockSpec((tm, ob.shape[1]), row), pl.BlockSpec((tm, oc.shape[1]), row),
                  full(w_gt), full(w_a), full(w_b), full(w_c), full(w_o), full(g), full(b)],
        out_specs=pl.BlockSpec((tm, D_MODEL), row),
        out_shape=jax.ShapeDtypeStruct((n, D_MODEL), F32),
        compiler_params=pltpu.CompilerParams(dimension_semantics=("parallel",), vmem_limit_bytes=VMEM_LIMIT_BYTES),
        name="merge",
    )(x2d, oa, ob, oc, w_gt, w_a, w_b, w_c, w_o, g, b)


def _router_gate(x, router):
    x_hi = x.astype(BF16)
    x_lo = (x - x_hi.astype(F32)).astype(BF16)
    r_hi = router.astype(BF16)
    r_lo = (router - r_hi.astype(F32)).astype(BF16)
    logits = _dot(x_hi, r_hi) + (_dot(x_hi, r_lo) + _dot(x_lo, r_hi))
    lane = lax.broadcasted_iota(jnp.int32, logits.shape, 1).astype(F32)
    lg = jnp.where(lane < N_EXPERTS, logits, NEG)
    m1 = jnp.max(lg, axis=-1, keepdims=True)
    i1 = jnp.min(jnp.where(lg == m1, lane, float(LANES)), axis=-1, keepdims=True)
    lg2 = jnp.where(lane == i1, NEG, lg)
    m2 = jnp.max(lg2, axis=-1, keepdims=True)
    i2 = jnp.min(jnp.where(lg2 == m2, lane, float(LANES)), axis=-1, keepdims=True)
    e2 = jnp.exp(m2 - m1)
    den = 1.0 + e2
    return jnp.where(lane == i1, 1.0 / den, 0.0) + jnp.where(lane == i2, e2 / den, 0.0)


def _ffn_kernel(*refs, alpha, moe):
    if moe:
        x_ref, router_ref, wg_ref, wu_ref, wd_ref, g_ref, b_ref, o_ref, acc_scr, xb_scr, gate_scr = refs
    else:
        x_ref, wg_ref, wu_ref, wd_ref, g_ref, b_ref, o_ref, acc_scr, xb_scr = refs
    e = pl.program_id(1)
    c = pl.program_id(2)
    first = jnp.logical_and(e == 0, c == 0)
    last = jnp.logical_and(e == pl.num_programs(1) - 1, c == pl.num_programs(2) - 1)

    @pl.when(first)
    def _init():
        x = x_ref[...]
        acc_scr[...] = jnp.zeros(acc_scr.shape, F32)
        xb_scr[...] = x.astype(BF16)
        if moe:
            gate_scr[...] = _router_gate(x, router_ref[...])

    xb = xb_scr[...]
    h = jax.nn.silu(_dot(xb, wg_ref[0])) * _dot(xb, wu_ref[0])
    if moe:
        gate = gate_scr[...]
        lane = lax.broadcasted_iota(jnp.int32, gate.shape, 1)
        h = h * jnp.sum(jnp.where(lane == e, gate, 0.0), axis=-1, keepdims=True)
    acc_scr[...] += _dot(h.astype(BF16), wd_ref[0])

    @pl.when(last)
    def _fin():
        o_ref[...] = _layer_norm(alpha * x_ref[...] + acc_scr[...], g_ref[...], b_ref[...])


def _ffn(x2d, router, wg, wu, wd, g, b, *, tm, fc, alpha):
    n = x2d.shape[0]
    n_e, _, f = wg.shape
    moe = router is not None
    row = lambda t, e, c: (t, 0)
    const = lambda t, e, c: (0, 0)
    in_specs = [pl.BlockSpec((tm, D_MODEL), row)]
    args = [x2d]
    if moe:
        in_specs.append(pl.BlockSpec(router.shape, const))
        args.append(router)
    in_specs += [pl.BlockSpec((1, D_MODEL, fc), lambda t, e, c: (e, 0, c)),
                 pl.BlockSpec((1, D_MODEL, fc), lambda t, e, c: (e, 0, c)),
                 pl.BlockSpec((1, fc, D_MODEL), lambda t, e, c: (e, c, 0)),
                 pl.BlockSpec(g.shape, const), pl.BlockSpec(b.shape, const)]
    args += [wg, wu, wd, g, b]
    scratch = [pltpu.VMEM((tm, D_MODEL), F32), pltpu.VMEM((tm, D_MODEL), BF16)]
    if moe:
        scratch.append(pltpu.VMEM((tm, LANES), F32))
    return pl.pallas_call(
        functools.partial(_ffn_kernel, alpha=alpha, moe=moe),
        grid=(n // tm, n_e, f // fc),
        in_specs=in_specs,
        out_specs=pl.BlockSpec((tm, D_MODEL), row),
        out_shape=jax.ShapeDtypeStruct((n, D_MODEL), F32),
        scratch_shapes=scratch,
        compiler_params=pltpu.CompilerParams(dimension_semantics=("parallel", "arbitrary", "arbitrary"),
                                             vmem_limit_bytes=VMEM_LIMIT_BYTES),
        name="moe_ffn" if moe else "ffn",
    )(*args)


def _rope_tables(pos):
    posf = pos.astype(F32)[:, None]

    def angles(dim):
        inv = ROPE_THETA ** (-jnp.arange(0, dim, 2, dtype=F32) / dim)
        ang = posf * inv[None, :]
        return jnp.cos(ang), jnp.sin(ang)

    lane = jnp.arange(LANES)
    cos_a, sin_a = angles(MOBA_ROPE_DIM)
    ha = MOBA_ROPE_DIM // 2
    ia = (lane % MOBA_HEAD_DIM)[None, :]
    ca = jnp.where(ia < 2 * ha, cos_a[:, lane % ha], 1.0)
    s1a = jnp.where(ia < ha, -sin_a[:, lane % ha], 0.0)
    s2a = jnp.where((ia >= ha) & (ia < 2 * ha), sin_a[:, lane % ha], 0.0)
    cos_b, sin_b = angles(MLA_ROPE_DIM)
    hb = MLA_ROPE_DIM // 2
    ib = (lane % MLA_ROPE_DIM)[None, :]
    cb = cos_b[:, lane % hb]
    s1b = jnp.where(ib < hb, -sin_b[:, lane % hb], 0.0)
    s2b = jnp.where(ib >= hb, sin_b[:, lane % hb], 0.0)
    return jnp.stack([ca, s1a, s2a, cb, s1b, s2b]).astype(F32)


def _pack_layer_weights(w_in_l, w_uq_l, w_uk_l, w_uv_l, w_br_moba_l, w_br_mla_l):
    pts, acc = [], 0
    for w in IN_WIDTHS[:-1]:
        acc += w
        pts.append(acc)
    mq, mk, mv, cq, ckv, kr, dq, dk, dv, gt = jnp.split(w_in_l, pts, axis=1)
    order = jnp.array(MOBA_HEAD_ORDER)
    mq = mq.reshape(D_MODEL, MOBA_HEADS, MOBA_HEAD_DIM)[:, order, :].reshape(D_MODEL, -1) * (MOBA_SCALE * LOG2E)
    kr = jnp.pad(kr, ((0, 0), (0, LANES - MLA_ROPE_DIM)))
    w_proj = jnp.concatenate([mq, mk, mv, cq, ckv, kr, dq * (DIFF_SCALE * LOG2E), dk, dv], axis=1).astype(BF16)
    w_gt = gt.astype(BF16)

    qd = MLA_NOPE_DIM + MLA_ROPE_DIM
    uq = w_uq_l.reshape(MLA_Q_RANK, MLA_HEADS, qd)
    uq_nope = jnp.transpose(uq[:, :, :MLA_NOPE_DIM], (1, 0, 2))
    uk_t = jnp.transpose(w_uk_l, (1, 2, 0))
    w_abs = _fold(uq_nope, uk_t)
    uq_rope = jnp.transpose(uq[:, :, MLA_NOPE_DIM:], (1, 0, 2))
    uq_rope = jnp.pad(uq_rope, ((0, 0), (0, 0), (0, LANES - MLA_ROPE_DIM)))
    w_q2 = jnp.concatenate([w_abs, uq_rope], axis=2) * (MLA_SCALE * LOG2E)
    w_q2 = jnp.transpose(w_q2, (1, 0, 2)).reshape(MLA_Q_RANK, MLA_HEADS * 2 * LANES).astype(BF16)

    uv = jnp.transpose(w_uv_l, (1, 0, 2))
    br = w_br_mla_l.reshape(MLA_HEADS, MLA_V_DIM, D_MODEL)
    w_ob = _fold(uv, br).reshape(MLA_HEADS * MLA_KV_RANK, D_MODEL).astype(BF16)

    w_a = w_br_moba_l.reshape(MOBA_HEADS, MOBA_HEAD_DIM, D_MODEL)[order].reshape(-1, D_MODEL).astype(BF16)
    return w_proj, w_gt, w_q2, w_ob, w_a


def _pick_tile(n, pref):
    t = min(n, pref)
    while n % t:
        t //= 2
    return t


def _pick_ff_chunk(f, cap=1536):
    whole_tiles = [c for c in range(2 * MXU_TILE, min(f, cap // 2) + 1, MXU_TILE) if f % c == 0]
    if whole_tiles:
        return whole_tiles[-1]
    best = LANES
    for c in range(LANES, min(f, cap) + 1, LANES):
        if f % c == 0:
            best = c
    return best


def kernel(x_prompt, x_sample, cache_moba_k, cache_moba_v, cache_mla_ckv, cache_mla_krope, cache_diff_k, cache_diff_v, page_table, w_in, mla_q_norm, mla_w_uq, mla_kv_norm, mla_w_uk, mla_w_uv, diff_lambda_q1, diff_lambda_k1, diff_lambda_q2, diff_lambda_k2, diff_subln, w_br_moba, w_br_mla, w_br_diff, w_o, ln1_g, ln1_b, ln2_g, ln2_b, ffn_w_gate, ffn_w_up, ffn_w_down, moe_router, moe_w_gate, moe_w_up, moe_w_down):
    bsz, seq, _ = x_prompt.shape
    n_seq, dec_t, _ = x_sample.shape
    assert dec_t == 1 and seq % MOBA_BLOCK == 0
    depth = w_in.shape[0]
    n_pool = cache_moba_k.shape[1]
    n_pages = page_table.shape[1]
    past = n_pages * PAGE_SIZE
    alpha = (2 * depth) ** 0.25
    n_p = bsz * seq

    kt_moba = jnp.transpose(cache_moba_k, (0, 1, 3, 4, 2)).reshape(depth, n_pool, LANES, PAGE_SIZE)
    vt_moba = jnp.transpose(cache_moba_v, (0, 1, 3, 4, 2)).reshape(depth, n_pool, LANES, PAGE_SIZE)
    krt_mla = jnp.transpose(cache_mla_krope, (0, 1, 3, 2))
    kt_diff = jnp.transpose(cache_diff_k, (0, 1, 3, 4, 5, 2)).reshape(depth, n_pool, LANES, PAGE_SIZE)
    v_diff = cache_diff_v.reshape(depth, n_pool, PAGE_SIZE, LANES)

    tabs_p = _rope_tables(jnp.arange(seq, dtype=jnp.int32))
    tabs_s = jnp.broadcast_to(_rope_tables(past + jnp.arange(1, dtype=jnp.int32)), (6, n_seq, LANES))

    tm_p = _pick_tile(seq, 512)
    tq_big = _pick_tile(seq, 512)
    tm_ffn = _pick_tile(n_p, 512)

    xp = x_prompt.reshape(n_p, D_MODEL)
    xs = x_sample.reshape(n_seq, D_MODEL)
    leaves = [[] for _ in range(12)]

    for l in range(depth):
        lam_init = 0.8 - 0.6 * math.exp(-0.3 * l)
        w_proj, w_gt, w_q2, w_ob, w_a = _pack_layer_weights(
            w_in[l], mla_w_uq[l], mla_w_uk[l], mla_w_uv[l], w_br_moba[l], w_br_mla[l])
        w_c = w_br_diff[l].astype(BF16)
        w_out = w_o[l].astype(BF16)
        qn = mla_q_norm[l][None, :]
        kvn = mla_kv_norm[l][None, :]
        lam_rows = jnp.concatenate([jnp.stack([diff_lambda_q1[l], diff_lambda_k1[l], diff_lambda_q2[l], diff_lambda_k2[l]]),
                                    jnp.zeros((4, DIFF_HEAD_DIM), F32)]).astype(F32)
        subln = diff_subln[l][None, :]

        (mq, mk, mv, mkb, mvt, qcat, ckv, kr, kcat, ckvt, dq, dk, dv, dkb, dvt, kmean) = _in_proj(
            xp, w_proj, w_q2, qn, kvn, tabs_p, tm=tm_p, with_kmean=True)
        nb = seq // MOBA_BLOCK
        kmean_b = jnp.pad(kmean.reshape(bsz, nb, LANES), ((0, 0), (0, LANES - nb), (0, 0))).astype(BF16)
        r3 = lambda a: a.reshape(bsz, seq, a.shape[1])
        oa_p = _flash(r3(mq), r3(mkb), mvt, (kmean_b,), mode="moba", tq=tq_big)
        ob_p = _flash(r3(qcat), r3(kcat), ckvt, (), mode="mla", tq=tq_big)
        oc_p = _flash(r3(dq), r3(dkb), dvt, (lam_rows, subln), mode="diff", tq=tq_big, lam_init=lam_init)

        (mq_s, mk_s, mv_s, _, _, qcat_s, ckv_s, kr_s, _, _, dq_s, dk_s, dv_s, _, _) = _in_proj(
            xs, w_proj, w_q2, qn, kvn, tabs_s, tm=n_seq, with_kmean=False)
        q3 = lambda a: a.reshape(n_seq, N_STREAMS, a.shape[1] // N_STREAMS)
        r1 = lambda a: a[:, None, :]
        kcat_s = jnp.concatenate([ckv_s, kr_s, jnp.zeros((n_seq, LANES - MLA_ROPE_DIM), F32)], axis=1)
        oa_s = _decode(page_table, q3(mq_s), r1(mk_s), r1(mv_s), (), kt_moba, vt_moba, mode="moba", layer=l)
        ob_s = _decode(page_table, q3(qcat_s), r1(kcat_s), r1(ckv_s), (), cache_mla_ckv, krt_mla, mode="mla", layer=l)
        oc_s = _decode(page_table, q3(dq_s), r1(dk_s), r1(dv_s), (lam_rows, subln), kt_diff, v_diff,
                       mode="diff", layer=l, lam_init=lam_init)

        g1, b1 = ln1_g[l][None, :], ln1_b[l][None, :]
        g2, b2 = ln2_g[l][None, :], ln2_b[l][None, :]
        xp = _merge(xp, oa_p.reshape(n_p, -1), ob_p.reshape(n_p, -1), oc_p.reshape(n_p, -1),
                    w_gt, w_a, w_ob, w_c, w_out, g1, b1, tm=tm_p, alpha=alpha)
        xs = _merge(xs, oa_s.reshape(n_seq, -1), ob_s.reshape(n_seq, -1), oc_s.reshape(n_seq, -1),
                    w_gt, w_a, w_ob, w_c, w_out, g1, b1, tm=n_seq, alpha=alpha)
        j = l // 2
        if l % 2 == 0:
            router = None
            wg, wu, wd = (ffn_w_gate[j][None].astype(BF16), ffn_w_up[j][None].astype(BF16),
                          ffn_w_down[j][None].astype(BF16))
        else:
            router = jnp.pad(moe_router[j], ((0, 0), (0, LANES - N_EXPERTS)))
            wg, wu, wd = moe_w_gate[j].astype(BF16), moe_w_up[j].astype(BF16), moe_w_down[j].astype(BF16)
        fc = _pick_ff_chunk(wg.shape[2])
        tm_mix = _pick_tile(n_p, 1024) if fc <= 2 * MXU_TILE else tm_ffn
        xp = _ffn(xp, router, wg, wu, wd, g2, b2, tm=tm_mix, fc=fc, alpha=alpha)
        xs = _ffn(xs, router, wg, wu, wd, g2, b2, tm=n_seq, fc=fc, alpha=alpha)

        p5 = lambda a, *tail: a.reshape(bsz, seq, *tail)
        s5 = lambda a, *tail: a.reshape(n_seq, 1, *tail)
        new = [p5(mk, MOBA_KV_HEADS, MOBA_HEAD_DIM), s5(mk_s, MOBA_KV_HEADS, MOBA_HEAD_DIM),
               p5(mv, MOBA_KV_HEADS, MOBA_HEAD_DIM), s5(mv_s, MOBA_KV_HEADS, MOBA_HEAD_DIM),
               p5(ckv, MLA_KV_RANK), s5(ckv_s, MLA_KV_RANK),
               p5(kr, MLA_ROPE_DIM), s5(kr_s, MLA_ROPE_DIM),
               p5(dk, DIFF_KV_HEADS, 2, DIFF_HEAD_DIM), s5(dk_s, DIFF_KV_HEADS, 2, DIFF_HEAD_DIM),
               p5(dv, DIFF_KV_HEADS, 2 * DIFF_HEAD_DIM), s5(dv_s, DIFF_KV_HEADS, 2 * DIFF_HEAD_DIM)]
        for acc_list, a in zip(leaves, new):
            acc_list.append(a)

    return (xp.reshape(bsz, seq, D_MODEL), xs.reshape(n_seq, 1, D_MODEL)) + tuple(jnp.stack(a) for a in leaves)
```
